```python
import jax, jax.numpy as jnp
from jax import lax
import numpy as np

D_MODEL = 1024
BATCH = 1
SEQ = 16384
DEPTH = 1
DEC_BATCH = 32
DEC_SEQ = 16
PAST_LEN = 4096

CHUNK = 64
N_META = 16
RET_WIDTH = D_MODEL // 2
RET_HEADS = 4
HEAD_DIM = RET_WIDTH // RET_HEADS
CONV_WIDTH = D_MODEL - RET_WIDTH
CONV_K = 31
D_FF = 4 * D_MODEL
N_IN = 4 * RET_WIDTH + 2 * CONV_WIDTH
EPS = 1e-6
ROPE_BASE = 10000.0

kernel_name = "hymba_retention_conformer_stream"


def rms_norm(x, g):
    xf = x.astype(jnp.float32)
    y = xf * lax.rsqrt(jnp.mean(xf * xf, axis=-1, keepdims=True) + EPS)
    return (y * g.astype(jnp.float32)).astype(x.dtype)


def layer_norm(x, g, b):
    xf = x.astype(jnp.float32)
    mu = jnp.mean(xf, axis=-1, keepdims=True)
    var = jnp.mean(jnp.square(xf - mu), axis=-1, keepdims=True)
    y = (xf - mu) * lax.rsqrt(var + EPS)
    return (y * g.astype(jnp.float32) + b.astype(jnp.float32)).astype(x.dtype)


def rope(x, pos):
    half = HEAD_DIM // 2
    inv = ROPE_BASE ** (-jnp.arange(half, dtype=jnp.float32) / half)
    ang = pos.astype(jnp.float32)[:, None] * inv[None, :]
    cos = jnp.cos(ang)[None, :, None, :]
    sin = jnp.sin(ang)[None, :, None, :]
    xf = x.astype(jnp.float32)
    x1, x2 = xf[..., :half], xf[..., half:]
    return jnp.concatenate([x1 * cos - x2 * sin, x2 * cos + x1 * sin], axis=-1).astype(x.dtype)


def log_gamma():
    return jnp.log1p(-jnp.exp2(-5.0 - jnp.arange(RET_HEADS, dtype=jnp.float32)))


def retention_block(q, k, v, S):
    q = q.astype(jnp.float32)
    k = k.astype(jnp.float32)
    v = v.astype(jnp.float32)
    S = S.astype(jnp.float32)
    L = q.shape[1]
    lg = log_gamma()
    idx = jnp.arange(L, dtype=jnp.float32)
    d_intra = jnp.exp(lg[:, None, None] * jnp.abs(idx[:, None] - idx[None, :]))
    d_in = jnp.exp(lg[None, :] * (idx[:, None] + 1.0))
    d_out = jnp.exp(lg[:, None] * (L - 1.0 - idx[None, :]))
    scores = jnp.einsum('bihd,bjhd->bhij', q, k) * d_intra[None]
    intra = jnp.einsum('bhij,bjhe->bihe', scores, v)
    inter = jnp.einsum('bihd,bhde->bihe', q, S) * d_in[None, :, :, None]
    S_new = S * jnp.exp(lg * L)[None, :, None, None] + jnp.einsum('bjhd,bjhe,hj->bhde', k, v, d_out)
    return intra + inter, S_new


def retention_prompt(q, k, v):
    B, T = q.shape[:2]
    n_pad = (-T) % CHUNK
    padw = ((0, 0), (n_pad, 0), (0, 0), (0, 0))
    nc = (T + n_pad) // CHUNK

    def blocks(a):
        a = jnp.pad(a.astype(jnp.float32), padw)
        return a.reshape(B, nc, CHUNK, RET_HEADS, HEAD_DIM).transpose(1, 0, 2, 3, 4)

    def body(S, blk):
        qb, kb, vb = blk
        o, S = retention_block(qb, kb, vb, S)
        return S, o

    S0 = jnp.zeros((B, RET_HEADS, HEAD_DIM, HEAD_DIM), jnp.float32)
    S_fin, o = lax.scan(body, S0, (blocks(q), blocks(k), blocks(v)))
    o = o.transpose(1, 0, 2, 3, 4).reshape(B, nc * CHUNK, RET_HEADS, HEAD_DIM)[:, n_pad:]
    return o, S_fin


def mixer_inputs(x, pos, g_pre, w_in):
    B, T = x.shape[:2]
    h = rms_norm(x, g_pre)
    proj = h @ w_in
    q, k, v, gate, ga, gb = jnp.split(
        proj, [RET_WIDTH, 2 * RET_WIDTH, 3 * RET_WIDTH, 4 * RET_WIDTH, 4 * RET_WIDTH + CONV_WIDTH], axis=-1)
    q = rope(q.reshape(B, T, RET_HEADS, HEAD_DIM), pos)
    k = rope(k.reshape(B, T, RET_HEADS, HEAD_DIM), pos) * (HEAD_DIM ** -0.5)
    v = v.reshape(B, T, RET_HEADS, HEAD_DIM)
    u = ga * jax.nn.sigmoid(gb)
    return q, k, v, gate, u


def conv_module(u, buf, dw_w, dw_b, ln_g, ln_b):
    ext = jnp.concatenate([buf.astype(u.dtype), u], axis=1)
    y = lax.conv_general_dilated(ext, dw_w[:, None, :].astype(u.dtype), window_strides=(1,), padding='VALID',
                                 dimension_numbers=('NWC', 'WIO', 'NWC'),
                                 feature_group_count=CONV_WIDTH) + dw_b
    y = jax.nn.silu(layer_norm(y, ln_g, ln_b))
    return y, ext[:, -(CONV_K - 1):]


def mixer_outputs(x, ret, gate, u, conv_buf, gn_g, gn_b, dw_w, dw_b, cln_g, cln_b, w_out, g_post_mix,
                  g_pre_mlp, w_mlp_in, w_mlp_out, g_post_mlp):
    B, T = x.shape[:2]
    mu = jnp.mean(ret, axis=-1, keepdims=True)
    var = jnp.mean(jnp.square(ret - mu), axis=-1, keepdims=True)
    rn = ((ret - mu) * lax.rsqrt(var + EPS)).reshape(B, T, RET_WIDTH)
    rn = (rn * gn_g.astype(jnp.float32) + gn_b.astype(jnp.float32)).astype(x.dtype)
    ret_out = rn * jax.nn.silu(gate)
    conv_out, new_buf = conv_module(u, conv_buf, dw_w, dw_b, cln_g, cln_b)
    mix = jnp.concatenate([ret_out, conv_out.astype(x.dtype)], axis=-1) @ w_out
    x = x + rms_norm(mix, g_post_mix)
    h = rms_norm(x, g_pre_mlp)
    f = jnp.square(jax.nn.relu(h @ w_mlp_in)) @ w_mlp_out
    x = x + rms_norm(f, g_post_mlp)
    return x, new_buf


def setup_inputs(seed: int = 0) -> dict:
    key = jax.random.key(seed)
    ks = jax.random.split(key, 19)
    f32 = jnp.float32

    def nrm(k, shape, scale):
        return jax.random.normal(k, shape, f32) * scale

    def gain(k, shape):
        return 1.0 + 0.05 * jax.random.normal(k, shape, f32)

    return {
        "x_prompt": nrm(ks[0], (BATCH, SEQ, D_MODEL), 1.0),
        "x_sample": nrm(ks[1], (DEC_BATCH, DEC_SEQ, D_MODEL), 1.0),
        "state_ret": nrm(ks[2], (DEPTH, DEC_BATCH, RET_HEADS, HEAD_DIM, HEAD_DIM), 0.5),
        "cache_conv": nrm(ks[3], (DEPTH, DEC_BATCH, CONV_K - 1, CONV_WIDTH), 0.5),
        "meta": nrm(ks[4], (N_META, D_MODEL), 1.0),
        "g_pre_mix": gain(ks[5], (DEPTH, D_MODEL)),
        "w_in": nrm(ks[6], (DEPTH, D_MODEL, N_IN), D_MODEL ** -0.5),
        "gn_g": gain(ks[7], (DEPTH, RET_WIDTH)),
        "gn_b": nrm(ks[8], (DEPTH, RET_WIDTH), 0.02),
        "dw_w": nrm(ks[9], (DEPTH, CONV_K, CONV_WIDTH), CONV_K ** -0.5),
        "dw_b": nrm(ks[10], (DEPTH, CONV_WIDTH), 0.02),
        "cln_g": gain(ks[11], (DEPTH, CONV_WIDTH)),
        "cln_b": nrm(ks[12], (DEPTH, CONV_WIDTH), 0.02),
        "w_out": nrm(ks[13], (DEPTH, RET_WIDTH + CONV_WIDTH, D_MODEL), (RET_WIDTH + CONV_WIDTH) ** -0.5),
        "g_post_mix": gain(ks[14], (DEPTH, D_MODEL)),
        "g_pre_mlp": gain(ks[15], (DEPTH, D_MODEL)),
        "w_mlp_in": nrm(ks[16], (DEPTH, D_MODEL, D_FF), D_MODEL ** -0.5),
        "w_mlp_out": nrm(ks[17], (DEPTH, D_FF, D_MODEL), D_FF ** -0.5),
        "g_post_mlp": gain(ks[18], (DEPTH, D_MODEL)),
    }


def reference(x_prompt, x_sample, state_ret, cache_conv, meta, g_pre_mix, w_in, gn_g, gn_b, dw_w, dw_b,
              cln_g, cln_b, w_out, g_post_mix, g_pre_mlp, w_mlp_in, w_mlp_out, g_post_mlp):
    B = x_prompt.shape[0]
    meta_b = jnp.broadcast_to(meta[None].astype(x_prompt.dtype), (B, N_META, D_MODEL))
    xp = jnp.concatenate([meta_b, x_prompt], axis=1)
    xs = x_sample
    pos_p = jnp.arange(xp.shape[1], dtype=jnp.int32)
    pos_s = N_META + PAST_LEN + jnp.arange(xs.shape[1], dtype=jnp.int32)
    ret_p_list, conv_p_list, ret_s_list, conv_s_list = [], [], [], []
    for l in range(DEPTH):
        wts = (gn_g[l], gn_b[l], dw_w[l], dw_b[l], cln_g[l], cln_b[l], w_out[l], g_post_mix[l],
               g_pre_mlp[l], w_mlp_in[l], w_mlp_out[l], g_post_mlp[l])
        q, k, v, gate, u = mixer_inputs(xp, pos_p, g_pre_mix[l], w_in[l])
        ret, S_p = retention_prompt(q, k, v)
        zero_buf = jnp.zeros((B, CONV_K - 1, CONV_WIDTH), u.dtype)
        xp, buf_p = mixer_outputs(xp, ret, gate, u, zero_buf, *wts)
        q, k, v, gate, u = mixer_inputs(xs, pos_s, g_pre_mix[l], w_in[l])
        ret, S_s = retention_block(q, k, v, state_ret[l])
        xs, buf_s = mixer_outputs(xs, ret, gate, u, cache_conv[l], *wts)
        ret_p_list.append(S_p.astype(x_prompt.dtype))
        conv_p_list.append(buf_p)
        ret_s_list.append(S_s.astype(x_sample.dtype))
        conv_s_list.append(buf_s)
    y_prompt = xp[:, N_META:]
    return (y_prompt, xs, jnp.stack(ret_p_list), jnp.stack(conv_p_list), jnp.stack(ret_s_list), jnp.stack(conv_s_list))
```

```python
import functools

import jax
import jax.numpy as jnp
from jax import lax
from jax.experimental import pallas as pl
from jax.experimental.pallas import tpu as pltpu

D_MODEL = 1024
CHUNK = 64
N_META = 16
RET_WIDTH = 512
RET_HEADS = 4
HEAD_DIM = 128
CONV_WIDTH = 512
CONV_K = 31
D_FF = 4096
N_IN = 4 * RET_WIDTH + 2 * CONV_WIDTH
EPS = 1e-6
ROPE_BASE = 10000.0

HIST = 32
HIST_OFF = HIST - (CONV_K - 1)
PROMPT_TILE = 256
GROUP_ROWS = 128
STREAM_ROWS = 16
GROUP_STREAMS = GROUP_ROWS // STREAM_ROWS
MLP_TILE = 512
CONV_ROW_BLOCK = 32
VMEM_LIMIT_BYTES = 56 * 1024 * 1024

_F32 = jnp.float32
_BF16 = jnp.bfloat16


def _rms(x, g):
    return x * lax.rsqrt(jnp.mean(x * x, axis=-1, keepdims=True) + EPS) * g


def _silu(x):
    return x * jax.nn.sigmoid(x)


def _dot(a, b):
    return jnp.dot(a, b, preferred_element_type=_F32)


def _dot_nt(a, b):
    return lax.dot_general(a, b, (((1,), (1,)), ((), ())), preferred_element_type=_F32)


def _dot_tn(a, b):
    return lax.dot_general(a, b, (((0,), (0,)), ((), ())), preferred_element_type=_F32)


def _head(x, h):
    return x[:, h * HEAD_DIM:(h + 1) * HEAD_DIM]


def _rope(xh, cos_f, sin_f):
    return xh * cos_f + pltpu.roll(xh, HEAD_DIM // 2, 1) * sin_f


def _project(x, g_pre, w_in, cos_f, sin_f):
    h = _rms(x, g_pre).astype(_BF16)
    proj = _dot(h, w_in)
    q = proj[:, 0:RET_WIDTH]
    k = proj[:, RET_WIDTH:2 * RET_WIDTH]
    v = proj[:, 2 * RET_WIDTH:3 * RET_WIDTH]
    gate = proj[:, 3 * RET_WIDTH:4 * RET_WIDTH]
    ga = proj[:, 4 * RET_WIDTH:4 * RET_WIDTH + CONV_WIDTH]
    gb = proj[:, 4 * RET_WIDTH + CONV_WIDTH:N_IN]
    qs = [_rope(_head(q, hh), cos_f, sin_f) for hh in range(RET_HEADS)]
    ks = [_rope(_head(k, hh), cos_f, sin_f) * (HEAD_DIM ** -0.5) for hh in range(RET_HEADS)]
    vs = [_head(v, hh) for hh in range(RET_HEADS)]
    u = ga * jax.nn.sigmoid(gb)
    return qs, ks, vs, gate, u


def _group_norm_gate(ret_h, gate_h, g, b):
    mu = jnp.mean(ret_h, axis=-1, keepdims=True)
    d = ret_h - mu
    var = jnp.mean(d * d, axis=-1, keepdims=True)
    rn = d * lax.rsqrt(var + EPS) * g + b
    return rn * _silu(gate_h)


def _layer_norm(x, g, b):
    mu = jnp.mean(x, axis=-1, keepdims=True)
    d = x - mu
    var = jnp.mean(d * d, axis=-1, keepdims=True)
    return d * lax.rsqrt(var + EPS) * g + b


def _causal_conv(ext_ref, dww_ref, dwb_ref, y_ref, n_rows):
    rb = min(CONV_ROW_BLOCK, n_rows)
    for c in range(CONV_WIDTH // 128):
        lanes = slice(c * 128, (c + 1) * 128)
        taps = [dww_ref[k:k + 1, lanes] for k in range(CONV_K)]
        bias = dwb_ref[:, lanes]
        for r0 in range(0, n_rows, rb):
            acc = ext_ref[c, r0 + HIST_OFF:r0 + HIST_OFF + rb, :] * taps[0]
            for k in range(1, CONV_K):
                acc = acc + ext_ref[c, r0 + HIST_OFF + k:r0 + HIST_OFF + k + rb, :] * taps[k]
            y_ref[r0:r0 + rb, lanes] = acc + bias


def _ext_store(ext_ref, r0, rows_value):
    n = rows_value.shape[0]
    for c in range(CONV_WIDTH // 128):
        ext_ref[c, r0:r0 + n, :] = rows_value[:, c * 128:(c + 1) * 128]


def _ext_load(ext_ref, r0, n):
    return jnp.concatenate([ext_ref[c, r0:r0 + n, :] for c in range(CONV_WIDTH // 128)], axis=-1)


def _mix_out(x, ret_out, conv_out, w_out, g_post):
    cat = jnp.concatenate([ret_out.astype(_BF16), conv_out.astype(_BF16)], axis=-1)
    return x + _rms(_dot(cat, w_out), g_post)


def _prompt_mixer_kernel(x_ref, cos_ref, sin_ref, gpre_ref, win_ref, dmat_ref, din_ref, dout_ref, sdec_ref,
                         gng_ref, gnb_ref, dww_ref, dwb_ref, clng_ref, clnb_ref, wout_ref, gpost_ref,
                         s0_ref, c0_ref,
                         x1_ref, sfin_ref, cfin_ref,
                         s_scr, ext_scr, y_scr):
    i = pl.program_id(0)
    rows = x_ref.shape[0]

    @pl.when(i == 0)
    def _():
        s_scr[...] = s0_ref[...]
        _ext_store(ext_scr, 0, jnp.zeros((HIST_OFF, CONV_WIDTH), _F32))
        _ext_store(ext_scr, HIST_OFF, c0_ref[...])

    x = x_ref[...]
    qs, ks, vs, gate, u = _project(x, gpre_ref[...], win_ref[...], cos_ref[...], sin_ref[...])

    ret_parts = []
    for h in range(RET_HEADS):
        hl = slice(h * HEAD_DIM, (h + 1) * HEAD_DIM)
        qb = qs[h].astype(_BF16)
        vb = vs[h].astype(_BF16)
        p = (_dot_nt(qb, ks[h].astype(_BF16)) * dmat_ref[h]).astype(_BF16)
        intra = _dot(p, vb)
        s_old = s_scr[h]
        inter = _dot(qb, s_old.astype(_BF16)) * din_ref[:, hl]
        kd = (ks[h] * dout_ref[:, hl]).astype(_BF16)
        s_scr[h] = s_old * sdec_ref[h:h + 1, :] + _dot_tn(kd, vb)
        ret_parts.append(_group_norm_gate(intra + inter, _head(gate, h), gng_ref[:, hl], gnb_ref[:, hl]))
    ret_out = jnp.concatenate(ret_parts, axis=-1)

    _ext_store(ext_scr, HIST, u)
    _causal_conv(ext_scr, dww_ref, dwb_ref, y_scr, rows)
    ext_scr[:, 0:HIST, :] = ext_scr[:, rows:rows + HIST, :]
    conv_out = _silu(_layer_norm(y_scr[...], clng_ref[...], clnb_ref[...]))

    x1_ref[...] = _mix_out(x, ret_out, conv_out, wout_ref[...], gpost_ref[...])

    @pl.when(i == pl.num_programs(0) - 1)
    def _():
        sfin_ref[...] = s_scr[...]
        cfin_ref[...] = _ext_load(ext_scr, HIST_OFF, CONV_K - 1)


def _const_spec(shape):
    zeros = (0,) * len(shape)
    return pl.BlockSpec(shape, lambda i, z=zeros: z)


def _prompt_mixer(x, cos_f, sin_f, g_pre, w_in, dmat, d_in, d_out, s_dec, gn_g, gn_b, dw_w, dw_b, cln_g, cln_b,
                  w_out, g_post, s0, c0):
    n = x.shape[0]
    t = PROMPT_TILE
    consts = (g_pre, w_in, dmat, d_in, d_out, s_dec, gn_g, gn_b, dw_w, dw_b, cln_g, cln_b, w_out, g_post, s0, c0)
    in_specs = [pl.BlockSpec((t, D_MODEL), lambda i: (i, 0)),
                pl.BlockSpec((t, HEAD_DIM), lambda i: (i, 0)),
                pl.BlockSpec((t, HEAD_DIM), lambda i: (i, 0))] + [_const_spec(c.shape) for c in consts]
    out_shape = (jax.ShapeDtypeStruct((n, D_MODEL), _F32),
                 jax.ShapeDtypeStruct((RET_HEADS, HEAD_DIM, HEAD_DIM), _F32),
                 jax.ShapeDtypeStruct((CONV_K - 1, CONV_WIDTH), _F32))
    out_specs = (pl.BlockSpec((t, D_MODEL), lambda i: (i, 0)),
                 _const_spec((RET_HEADS, HEAD_DIM, HEAD_DIM)),
                 _const_spec((CONV_K - 1, CONV_WIDTH)))
    return pl.pallas_call(
        _prompt_mixer_kernel,
        grid=(n // t,),
        in_specs=in_specs,
        out_specs=out_specs,
        out_shape=out_shape,
        scratch_shapes=[pltpu.VMEM((RET_HEADS, HEAD_DIM, HEAD_DIM), _F32),
                        pltpu.VMEM((CONV_WIDTH // 128, HIST + t, 128), _F32),
                        pltpu.VMEM((t, CONV_WIDTH), _F32)],
        compiler_params=pltpu.CompilerParams(dimension_semantics=("arbitrary",),
                                             vmem_limit_bytes=VMEM_LIMIT_BYTES),
        name="prompt_mixer",
    )(x, cos_f, sin_f, *consts)


def _sample_mixer_kernel(x_ref, cos_ref, sin_ref, gpre_ref, win_ref, dmat_ref, din_ref, dout_ref, sdec_ref,
                         gng_ref, gnb_ref, dww_ref, dwb_ref, clng_ref, clnb_ref, wout_ref, gpost_ref,
                         state_ref, cache_ref,
                         x1_ref, snew_ref, cnew_ref, smeta_ref, cmeta_ref,
                         ext_scr, y_scr):
    g = pl.program_id(0)
    has_state = g > 0

    x = x_ref[...]
    qs, ks, vs, gate, u = _project(x, gpre_ref[...], win_ref[...], cos_ref[...], sin_ref[...])

    ret_parts = []
    for h in range(RET_HEADS):
        hl = slice(h * HEAD_DIM, (h + 1) * HEAD_DIM)
        qb = qs[h].astype(_BF16)
        vb = vs[h].astype(_BF16)
        p = (_dot_nt(qb, ks[h].astype(_BF16)) * dmat_ref[h]).astype(_BF16)
        intra = _dot(p, vb)
        kd = (ks[h] * dout_ref[:, hl]).astype(_BF16)
        inter_parts = []
        for s in range(GROUP_STREAMS):
            rs = slice(s * STREAM_ROWS, (s + 1) * STREAM_ROWS)
            s_old = jnp.where(has_state, state_ref[s, h], 0.0)
            inter_parts.append(_dot(qb[rs], s_old.astype(_BF16)))
            snew_ref[s, h] = s_old * sdec_ref[h:h + 1, :] + _dot_tn(kd[rs], vb[rs])
        inter = jnp.concatenate(inter_parts, axis=0) * din_ref[:, hl]
        ret_parts.append(_group_norm_gate(intra + inter, _head(gate, h), gng_ref[:, hl], gnb_ref[:, hl]))
    ret_out = jnp.concatenate(ret_parts, axis=-1)

    for s in range(GROUP_STREAMS):
        rs = slice(s * STREAM_ROWS, (s + 1) * STREAM_ROWS)
        _ext_store(ext_scr, 0, jnp.zeros((HIST_OFF, CONV_WIDTH), _F32))
        _ext_store(ext_scr, HIST_OFF, jnp.where(has_state, cache_ref[s], 0.0))
        _ext_store(ext_scr, HIST, u[rs])
        _causal_conv(ext_scr, dww_ref, dwb_ref, y_scr.at[rs], STREAM_ROWS)
        cnew_ref[s] = _ext_load(ext_scr, HIST + STREAM_ROWS - (CONV_K - 1), CONV_K - 1)
    conv_out = _silu(_layer_norm(y_scr[...], clng_ref[...], clnb_ref[...]))

    x1_ref[...] = _mix_out(x, ret_out, conv_out, wout_ref[...], gpost_ref[...])

    @pl.when(g == 0)
    def _():
        smeta_ref[...] = snew_ref[0]
        cmeta_ref[...] = cnew_ref[0]


def _sample_mixer(x_all, cos_f, sin_f, g_pre, w_in, dmat, d_in, d_out, s_dec, gn_g, gn_b, dw_w, dw_b, cln_g, cln_b,
                  w_out, g_post, state, cache):
    n = x_all.shape[0]
    n_streams = state.shape[0]
    consts = (g_pre, w_in, dmat, d_in, d_out, s_dec, gn_g, gn_b, dw_w, dw_b, cln_g, cln_b, w_out, g_post)
    stream_blk = lambda i: (jnp.maximum(i - 1, 0), 0, 0, 0)
    cache_blk = lambda i: (jnp.maximum(i - 1, 0), 0, 0)
    in_specs = ([pl.BlockSpec((GROUP_ROWS, D_MODEL), lambda i: (i, 0)),
                 pl.BlockSpec((GROUP_ROWS, HEAD_DIM), lambda i: (i, 0)),
                 pl.BlockSpec((GROUP_ROWS, HEAD_DIM), lambda i: (i, 0))]
                + [_const_spec(c.shape) for c in consts]
                + [pl.BlockSpec((GROUP_STREAMS, RET_HEADS, HEAD_DIM, HEAD_DIM), stream_blk),
                   pl.BlockSpec((GROUP_STREAMS, CONV_K - 1, CONV_WIDTH), cache_blk)])
    out_shape = (jax.ShapeDtypeStruct((n, D_MODEL), _F32),
                 jax.ShapeDtypeStruct((n_streams, RET_HEADS, HEAD_DIM, HEAD_DIM), _F32),
                 jax.ShapeDtypeStruct((n_streams, CONV_K - 1, CONV_WIDTH), _F32),
                 jax.ShapeDtypeStruct((RET_HEADS, HEAD_DIM, HEAD_DIM), _F32),
                 jax.ShapeDtypeStruct((CONV_K - 1, CONV_WIDTH), _F32))
    out_specs = (pl.BlockSpec((GROUP_ROWS, D_MODEL), lambda i: (i, 0)),
                 pl.BlockSpec((GROUP_STREAMS, RET_HEADS, HEAD_DIM, HEAD_DIM), stream_blk),
                 pl.BlockSpec((GROUP_STREAMS, CONV_K - 1, CONV_WIDTH), cache_blk),
                 _const_spec((RET_HEADS, HEAD_DIM, HEAD_DIM)),
                 _const_spec((CONV_K - 1, CONV_WIDTH)))
    return pl.pallas_call(
        _sample_mixer_kernel,
        grid=(n // GROUP_ROWS,),
        in_specs=in_specs,
        out_specs=out_specs,
        out_shape=out_shape,
        scratch_shapes=[pltpu.VMEM((CONV_WIDTH // 128, HIST + STREAM_ROWS, 128), _F32),
                        pltpu.VMEM((GROUP_ROWS, CONV_WIDTH), _F32)],
        compiler_params=pltpu.CompilerParams(dimension_semantics=("arbitrary",),
                                             vmem_limit_bytes=VMEM_LIMIT_BYTES),
        name="sample_mixer",
    )(x_all, cos_f, sin_f, *consts, state, cache)


def _mlp_kernel(x_ref, gpre_ref, w1_ref, w2_ref, gpost_ref, y_ref):
    x = x_ref[...]
    h = _rms(x, gpre_ref[...]).astype(_BF16)
    a = jnp.maximum(_dot(h, w1_ref[...]), 0.0)
    f = _dot((a * a).astype(_BF16), w2_ref[...])
    y_ref[...] = x + _rms(f, gpost_ref[...])


def _mlp(x, g_pre, w1, w2, g_post, tile):
    n = x.shape[0]
    resident = functools.partial(pl.BlockSpec, pipeline_mode=pl.Buffered(1))
    return pl.pallas_call(
        _mlp_kernel,
        grid=(n // tile,),
        in_specs=[pl.BlockSpec((tile, D_MODEL), lambda i: (i, 0)),
                  _const_spec(g_pre.shape),
                  resident(w1.shape, lambda i: (0, 0)),
                  resident(w2.shape, lambda i: (0, 0)),
                  _const_spec(g_post.shape)],
        out_specs=pl.BlockSpec((tile, D_MODEL), lambda i: (i, 0)),
        out_shape=jax.ShapeDtypeStruct((n, D_MODEL), _F32),
        compiler_params=pltpu.CompilerParams(dimension_semantics=("arbitrary",),
                                             vmem_limit_bytes=VMEM_LIMIT_BYTES),
        name="mlp",
    )(x, g_pre, w1, w2, g_post)


def _log_gamma():
    return jnp.log1p(-jnp.exp2(-5.0 - jnp.arange(RET_HEADS, dtype=_F32)))


def _rope_tables(pos):
    half = HEAD_DIM // 2
    inv = ROPE_BASE ** (-jnp.arange(half, dtype=_F32) / half)
    ang = pos.astype(_F32)[:, None] * inv[None, :]
    cos = jnp.cos(ang)
    sin = jnp.sin(ang)
    return jnp.concatenate([cos, cos], axis=-1), jnp.concatenate([-sin, sin], axis=-1)


def _lanes(per_head):
    return jnp.repeat(per_head, HEAD_DIM, axis=1)


def _decay_tables(rows, segment, causal_chunk):
    lg = _log_gamma()
    idx = jnp.arange(rows)
    loc = (idx % segment).astype(_F32)
    same_seg = (idx[:, None] // segment) == (idx[None, :] // segment)
    visible = same_seg & ((idx[None, :] // causal_chunk) <= (idx[:, None] // causal_chunk))
    dist = jnp.abs(idx[:, None] - idx[None, :]).astype(_F32)
    dmat = jnp.where(visible[None], jnp.exp(lg[:, None, None] * dist[None]), 0.0)
    d_in = _lanes(jnp.exp(lg[None, :] * (loc[:, None] + 1.0)))
    d_out = _lanes(jnp.exp(lg[None, :] * (segment - 1.0 - loc[:, None])))
    s_dec = jnp.broadcast_to(jnp.exp(lg * segment)[:, None], (RET_HEADS, HEAD_DIM))
    return dmat, d_in, d_out, s_dec


def kernel(x_prompt, x_sample, state_ret, cache_conv, meta, g_pre_mix, w_in, gn_g, gn_b, dw_w, dw_b, cln_g, cln_b,
           w_out, g_post_mix, g_pre_mlp, w_mlp_in, w_mlp_out, g_post_mlp):
    batch, seq, _ = x_prompt.shape
    dec_batch, dec_seq, _ = x_sample.shape
    assert batch == 1 and dec_seq == STREAM_ROWS and meta.shape[0] == N_META == STREAM_ROWS
    assert seq % PROMPT_TILE == 0 and seq % MLP_TILE == 0 and dec_batch % GROUP_STREAMS == 0
    past_len = 4096
    depth = state_ret.shape[0]
    assert depth == 1

    l = 0
    mixer_w = (g_pre_mix[l][None], w_in[l].astype(_BF16))
    post_w = (gn_g[l][None], gn_b[l][None], dw_w[l], dw_b[l][None], cln_g[l][None], cln_b[l][None],
              w_out[l].astype(_BF16), g_post_mix[l][None])
    mlp_w = (g_pre_mlp[l][None], w_mlp_in[l].astype(_BF16), w_mlp_out[l].astype(_BF16), g_post_mlp[l][None])

    pad_rows = GROUP_ROWS - N_META
    x_all = jnp.concatenate([meta, jnp.zeros((pad_rows, D_MODEL), _F32),
                             x_sample.reshape(dec_batch * dec_seq, D_MODEL)], axis=0)
    pos_group0 = jnp.arange(GROUP_ROWS, dtype=jnp.int32)
    pos_stream = N_META + past_len + jnp.arange(dec_seq, dtype=jnp.int32)
    pos_all = jnp.concatenate([pos_group0, jnp.tile(pos_stream, dec_batch)])
    cos_s, sin_s = _rope_tables(pos_all)
    dec_s = _decay_tables(GROUP_ROWS, STREAM_ROWS, STREAM_ROWS)
    xs1, s_s, c_s, s_meta, c_meta = _sample_mixer(x_all, cos_s, sin_s, *mixer_w, *dec_s, *post_w,
                                                  state_ret[l], cache_conv[l])
    ys = _mlp(xs1, *mlp_w, tile=GROUP_ROWS * 5)[GROUP_ROWS:]

    cos_p, sin_p = _rope_tables(N_META + jnp.arange(seq, dtype=jnp.int32))
    dec_p = _decay_tables(PROMPT_TILE, PROMPT_TILE, CHUNK)
    xp1, s_p, c_p = _prompt_mixer(x_prompt[0], cos_p, sin_p, *mixer_w, *dec_p, *post_w, s_meta, c_meta)
    yp = _mlp(xp1, *mlp_w, tile=MLP_TILE)

    return (yp[None], ys.reshape(dec_batch, dec_seq, D_MODEL), s_p[None, None], c_p[None, None],
            s_s[None], c_s[None])
```

```python
import functools

import jax
import jax.numpy as jnp
from jax import lax
from jax.experimental import pallas as pl
from jax.experimental.pallas import tpu as pltpu

D_MODEL = 1024
CHUNK = 64
N_META = 16
RET_WIDTH = 512
RET_HEADS = 4
HEAD_DIM = 128
CONV_WIDTH = 512
CONV_K = 31
D_FF = 4096
N_IN = 4 * RET_WIDTH + 2 * CONV_WIDTH
EPS = 1e-6
ROPE_BASE = 10000.0

HIST = 32
HIST_OFF = HIST - (CONV_K - 1)
PROMPT_TILE = 256
GROUP_ROWS = 128
STREAM_ROWS = 16
GROUP_STREAMS = GROUP_ROWS // STREAM_ROWS
MLP_TILE = 512
CONV_ROW_BLOCK = 32
VMEM_LIMIT_BYTES = 56 * 1024 * 1024

_F32 = jnp.float32
_BF16 = jnp.bfloat16


def _rms(x, g):
    return x * lax.rsqrt(jnp.mean(x * x, axis=-1, keepdims=True) + EPS) * g


def _silu(x):
    return x * jax.nn.sigmoid(x)


def _dot(a, b):
    return jnp.dot(a, b, preferred_element_type=_F32)


def _dot_nt(a, b):
    return lax.dot_general(a, b, (((1,), (1,)), ((), ())), preferred_element_type=_F32)


def _dot_tn(a, b):
    return lax.dot_general(a, b, (((0,), (0,)), ((), ())), preferred_element_type=_F32)


def _head(x, h):
    return x[:, h * HEAD_DIM:(h + 1) * HEAD_DIM]


def _rope(xh, cos_f, sin_f):
    return xh * cos_f + pltpu.roll(xh, HEAD_DIM // 2, 1) * sin_f


def _project(x, g_pre, w_in, cos_f, sin_f):
    h = _rms(x, g_pre).astype(_BF16)
    proj = _dot(h, w_in)
    q = proj[:, 0:RET_WIDTH]
    k = proj[:, RET_WIDTH:2 * RET_WIDTH]
    v = proj[:, 2 * RET_WIDTH:3 * RET_WIDTH]
    gate = proj[:, 3 * RET_WIDTH:4 * RET_WIDTH]
    ga = proj[:, 4 * RET_WIDTH:4 * RET_WIDTH + CONV_WIDTH]
    gb = proj[:, 4 * RET_WIDTH + CONV_WIDTH:N_IN]
    qs = [_rope(_head(q, hh), cos_f, sin_f) for hh in range(RET_HEADS)]
    ks = [_rope(_head(k, hh), cos_f, sin_f) * (HEAD_DIM ** -0.5) for hh in range(RET_HEADS)]
    vs = [_head(v, hh) for hh in range(RET_HEADS)]
    u = ga * jax.nn.sigmoid(gb)
    return qs, ks, vs, gate, u


def _group_norm_gate(ret_h, gate_h, g, b):
    mu = jnp.mean(ret_h, axis=-1, keepdims=True)
    d = ret_h - mu
    var = jnp.mean(d * d, axis=-1, keepdims=True)
    rn = d * lax.rsqrt(var + EPS) * g + b
    return rn * _silu(gate_h)


def _layer_norm(x, g, b):
    mu = jnp.mean(x, axis=-1, keepdims=True)
    d = x - mu
    var = jnp.mean(d * d, axis=-1, keepdims=True)
    return d * lax.rsqrt(var + EPS) * g + b


def _causal_conv(ext_ref, dww_ref, dwb_ref, y_ref, n_rows):
    rb = min(CONV_ROW_BLOCK, n_rows)
    for c in range(CONV_WIDTH // 128):
        lanes = slice(c * 128, (c + 1) * 128)
        taps = [dww_ref[k:k + 1, lanes] for k in range(CONV_K)]
        bias = dwb_ref[:, lanes]
        for r0 in range(0, n_rows, rb):
            acc = ext_ref[c, r0 + HIST_OFF:r0 + HIST_OFF + rb, :] * taps[0]
            for k in range(1, CONV_K):
                acc = acc + ext_ref[c, r0 + HIST_OFF + k:r0 + HIST_OFF + k + rb, :] * taps[k]
            y_ref[r0:r0 + rb, lanes] = acc + bias


def _ext_store(ext_ref, r0, rows_value):
    n = rows_value.shape[0]
    for c in range(CONV_WIDTH // 128):
        ext_ref[c, r0:r0 + n, :] = rows_value[:, c * 128:(c + 1) * 128]


def _ext_load(ext_ref, r0, n):
    return jnp.concatenate([ext_ref[c, r0:r0 + n, :] for c in range(CONV_WIDTH // 128)], axis=-1)


def _mix_out(x, ret_out, conv_out, w_out, g_post):
    cat = jnp.concatenate([ret_out.astype(_BF16), conv_out.astype(_BF16)], axis=-1)
    return x + _rms(_dot(cat, w_out), g_post)


def _prompt_mixer_kernel(x_ref, tile_rot_ref, row_rot_ref, gpre_ref, win_ref, dmat_ref, din_ref, dout_ref, sdec_ref,
                         gng_ref, gnb_ref, dww_ref, dwb_ref, clng_ref, clnb_ref, wout_ref, gpost_ref,
                         s0_ref, c0_ref,
                         x1_ref, sfin_ref, cfin_ref,
                         s_scr, ext_scr, y_scr):
    i = pl.program_id(0)
    rows = x_ref.shape[0]

    @pl.when(i == 0)
    def _():
        s_scr[...] = s0_ref[...]
        _ext_store(ext_scr, 0, jnp.zeros((HIST_OFF, CONV_WIDTH), _F32))
        _ext_store(ext_scr, HIST_OFF, c0_ref[...])

    cos_a, sin_a = tile_rot_ref[0, 0, 0:1, :], tile_rot_ref[0, 1, 0:1, :]
    cos_f = cos_a * row_rot_ref[0] - sin_a * row_rot_ref[1]
    sin_f = sin_a * row_rot_ref[2] + cos_a * row_rot_ref[3]

    x = x_ref[...]
    qs, ks, vs, gate, u = _project(x, gpre_ref[...], win_ref[...], cos_f, sin_f)

    ret_parts = []
    for h in range(RET_HEADS):
        hl = slice(h * HEAD_DIM, (h + 1) * HEAD_DIM)
        qb = qs[h].astype(_BF16)
        vb = vs[h].astype(_BF16)
        p = (_dot_nt(qb, ks[h].astype(_BF16)) * dmat_ref[h]).astype(_BF16)
        intra = _dot(p, vb)
        s_old = s_scr[h]
        inter = _dot(qb, s_old.astype(_BF16)) * din_ref[:, hl]
        kd = (ks[h] * dout_ref[:, hl]).astype(_BF16)
        s_scr[h] = s_old * sdec_ref[h:h + 1, :] + _dot_tn(kd, vb)
        ret_parts.append(_group_norm_gate(intra + inter, _head(gate, h), gng_ref[:, hl], gnb_ref[:, hl]))
    ret_out = jnp.concatenate(ret_parts, axis=-1)

    _ext_store(ext_scr, HIST, u)
    _causal_conv(ext_scr, dww_ref, dwb_ref, y_scr, rows)
    ext_scr[:, 0:HIST, :] = ext_scr[:, rows:rows + HIST, :]
    conv_out = _silu(_layer_norm(y_scr[...], clng_ref[...], clnb_ref[...]))

    x1_ref[...] = _mix_out(x, ret_out, conv_out, wout_ref[...], gpost_ref[...])

    @pl.when(i == pl.num_programs(0) - 1)
    def _():
        sfin_ref[...] = s_scr[...]
        cfin_ref[...] = _ext_load(ext_scr, HIST_OFF, CONV_K - 1)


def _const_spec(shape):
    zeros = (0,) * len(shape)
    return pl.BlockSpec(shape, lambda i, z=zeros: z)


def _prompt_mixer(x, tile_rot, row_rot, g_pre, w_in, dmat, d_in, d_out, s_dec, gn_g, gn_b, dw_w, dw_b, cln_g, cln_b,
                  w_out, g_post, s0, c0):
    n = x.shape[0]
    t = PROMPT_TILE
    consts = (row_rot, g_pre, w_in, dmat, d_in, d_out, s_dec, gn_g, gn_b, dw_w, dw_b, cln_g, cln_b, w_out, g_post,
              s0, c0)
    in_specs = [pl.BlockSpec((t, D_MODEL), lambda i: (i, 0)),
                pl.BlockSpec((1,) + tile_rot.shape[1:], lambda i: (i, 0, 0, 0))] + [_const_spec(c.shape)
                                                                                    for c in consts]
    out_shape = (jax.ShapeDtypeStruct((n, D_MODEL), _F32),
                 jax.ShapeDtypeStruct((RET_HEADS, HEAD_DIM, HEAD_DIM), _F32),
                 jax.ShapeDtypeStruct((CONV_K - 1, CONV_WIDTH), _F32))
    out_specs = (pl.BlockSpec((t, D_MODEL), lambda i: (i, 0)),
                 _const_spec((RET_HEADS, HEAD_DIM, HEAD_DIM)),
                 _const_spec((CONV_K - 1, CONV_WIDTH)))
    return pl.pallas_call(
        _prompt_mixer_kernel,
        grid=(n // t,),
        in_specs=in_specs,
        out_specs=out_specs,
        out_shape=out_shape,
        scratch_shapes=[pltpu.VMEM((RET_HEADS, HEAD_DIM, HEAD_DIM), _F32),
                        pltpu.VMEM((CONV_WIDTH // 128, HIST + t, 128), _F32),
                        pltpu.VMEM((t, CONV_WIDTH), _F32)],
        compiler_params=pltpu.CompilerParams(dimension_semantics=("arbitrary",),
                                             vmem_limit_bytes=VMEM_LIMIT_BYTES),
        name="prompt_mixer",
    )(x, tile_rot, *consts)


def _sample_mixer_kernel(x_ref, cos_ref, sin_ref, gpre_ref, win_ref, dmat_ref, din_ref, dout_ref, sdec_ref,
                         gng_ref, gnb_ref, dww_ref, dwb_ref, clng_ref, clnb_ref, wout_ref, gpost_ref,
                         state_ref, cache_ref,
                         x1_ref, snew_ref, cnew_ref, smeta_ref, cmeta_ref,
                         ext_scr, y_scr):
    g = pl.program_id(0)
    has_state = g > 0

    x = x_ref[...]
    qs, ks, vs, gate, u = _project(x, gpre_ref[...], win_ref[...], cos_ref[...], sin_ref[...])

    ret_parts = []
    for h in range(RET_HEADS):
        hl = slice(h * HEAD_DIM, (h + 1) * HEAD_DIM)
        qb = qs[h].astype(_BF16)
        vb = vs[h].astype(_BF16)
        p = (_dot_nt(qb, ks[h].astype(_BF16)) * dmat_ref[h]).astype(_BF16)
        intra = _dot(p, vb)
        kd = (ks[h] * dout_ref[:, hl]).astype(_BF16)
        inter_parts = []
        for s in range(GROUP_STREAMS):
            rs = slice(s * STREAM_ROWS, (s + 1) * STREAM_ROWS)
            s_old = jnp.where(has_state, state_ref[s, h], 0.0)
            inter_parts.append(_dot(qb[rs], s_old.astype(_BF16)))
            snew_ref[s, h] = s_old * sdec_ref[h:h + 1, :] + _dot_tn(kd[rs], vb[rs])
        inter = jnp.concatenate(inter_parts, axis=0) * din_ref[:, hl]
        ret_parts.append(_group_norm_gate(intra + inter, _head(gate, h), gng_ref[:, hl], gnb_ref[:, hl]))
    ret_out = jnp.concatenate(ret_parts, axis=-1)

    for s in range(GROUP_STREAMS):
        rs = slice(s * STREAM_ROWS, (s + 1) * STREAM_ROWS)
        _ext_store(ext_scr, 0, jnp.zeros((HIST_OFF, CONV_WIDTH), _F32))
        _ext_store(ext_scr, HIST_OFF, jnp.where(has_state, cache_ref[s], 0.0))
        _ext_store(ext_scr, HIST, u[rs])
        _causal_conv(ext_scr, dww_ref, dwb_ref, y_scr.at[rs], STREAM_ROWS)
        cnew_ref[s] = _ext_load(ext_scr, HIST + STREAM_ROWS - (CONV_K - 1), CONV_K - 1)
    conv_out = _silu(_layer_norm(y_scr[...], clng_ref[...], clnb_ref[...]))

    x1_ref[...] = _mix_out(x, ret_out, conv_out, wout_ref[...], gpost_ref[...])

    @pl.when(g == 0)
    def _():
        smeta_ref[...] = snew_ref[0]
        cmeta_ref[...] = cnew_ref[0]


def _sample_mixer(x_all, cos_f, sin_f, g_pre, w_in, dmat, d_in, d_out, s_dec, gn_g, gn_b, dw_w, dw_b, cln_g, cln_b,
                  w_out, g_post, state, cache):
    n = x_all.shape[0]
    n_streams = state.shape[0]
    consts = (g_pre, w_in, dmat, d_in, d_out, s_dec, gn_g, gn_b, dw_w, dw_b, cln_g, cln_b, w_out, g_post)
    stream_blk = lambda i: (jnp.maximum(i - 1, 0), 0, 0, 0)
    cache_blk = lambda i: (jnp.maximum(i - 1, 0), 0, 0)
    in_specs = ([pl.BlockSpec((GROUP_ROWS, D_MODEL), lambda i: (i, 0)),
                 pl.BlockSpec((GROUP_ROWS, HEAD_DIM), lambda i: (i, 0)),
                 pl.BlockSpec((GROUP_ROWS, HEAD_DIM), lambda i: (i, 0))]
                + [_const_spec(c.shape) for c in consts]
                + [pl.BlockSpec((GROUP_STREAMS, RET_HEADS, HEAD_DIM, HEAD_DIM), stream_blk),
                   pl.BlockSpec((GROUP_STREAMS, CONV_K - 1, CONV_WIDTH), cache_blk)])
    out_shape = (jax.ShapeDtypeStruct((n, D_MODEL), _F32),
                 jax.ShapeDtypeStruct((n_streams, RET_HEADS, HEAD_DIM, HEAD_DIM), _F32),
                 jax.ShapeDtypeStruct((n_streams, CONV_K - 1, CONV_WIDTH), _F32),
                 jax.ShapeDtypeStruct((RET_HEADS, HEAD_DIM, HEAD_DIM), _F32),
                 jax.ShapeDtypeStruct((CONV_K - 1, CONV_WIDTH), _F32))
    out_specs = (pl.BlockSpec((GROUP_ROWS, D_MODEL), lambda i: (i, 0)),
                 pl.BlockSpec((GROUP_STREAMS, RET_HEADS, HEAD_DIM, HEAD_DIM), stream_blk),
                 pl.BlockSpec((GROUP_STREAMS, CONV_K - 1, CONV_WIDTH), cache_blk),
                 _const_spec((RET_HEADS, HEAD_DIM, HEAD_DIM)),
                 _const_spec((CONV_K - 1, CONV_WIDTH)))
    return pl.pallas_call(
        _sample_mixer_kernel,
        grid=(n // GROUP_ROWS,),
        in_specs=in_specs,
        out_specs=out_specs,
        out_shape=out_shape,
        scratch_shapes=[pltpu.VMEM((CONV_WIDTH // 128, HIST + STREAM_ROWS, 128), _F32),
                        pltpu.VMEM((GROUP_ROWS, CONV_WIDTH), _F32)],
        compiler_params=pltpu.CompilerParams(dimension_semantics=("arbitrary",),
                                             vmem_limit_bytes=VMEM_LIMIT_BYTES),
        name="sample_mixer",
    )(x_all, cos_f, sin_f, *consts, state, cache)


def _mlp_kernel(x_ref, gpre_ref, w1_ref, w2_ref, gpost_ref, y_ref):
    x = x_ref[...]
    h = _rms(x, gpre_ref[...]).astype(_BF16)
    a = jnp.maximum(_dot(h, w1_ref[...]), 0.0)
    f = _dot((a * a).astype(_BF16), w2_ref[...])
    y_ref[...] = x + _rms(f, gpost_ref[...])


def _mlp(x, g_pre, w1, w2, g_post, tile):
    n = x.shape[0]
    resident = functools.partial(pl.BlockSpec, pipeline_mode=pl.Buffered(1))
    return pl.pallas_call(
        _mlp_kernel,
        grid=(n // tile,),
        in_specs=[pl.BlockSpec((tile, D_MODEL), lambda i: (i, 0)),
                  _const_spec(g_pre.shape),
                  resident(w1.shape, lambda i: (0, 0)),
                  resident(w2.shape, lambda i: (0, 0)),
                  _const_spec(g_post.shape)],
        out_specs=pl.BlockSpec((tile, D_MODEL), lambda i: (i, 0)),
        out_shape=jax.ShapeDtypeStruct((n, D_MODEL), _F32),
        compiler_params=pltpu.CompilerParams(dimension_semantics=("arbitrary",),
                                             vmem_limit_bytes=VMEM_LIMIT_BYTES),
        name="mlp",
    )(x, g_pre, w1, w2, g_post)


def _log_gamma():
    return jnp.log1p(-jnp.exp2(-5.0 - jnp.arange(RET_HEADS, dtype=_F32)))


def _rope_tables(pos):
    half = HEAD_DIM // 2
    inv = ROPE_BASE ** (-jnp.arange(half, dtype=_F32) / half)
    ang = pos.astype(_F32)[:, None] * inv[None, :]
    cos = jnp.cos(ang)
    sin = jnp.sin(ang)
    return jnp.concatenate([cos, cos], axis=-1), jnp.concatenate([-sin, sin], axis=-1)


def _prompt_rope_tables(n_tiles, tile):
    half = HEAD_DIM // 2
    inv = ROPE_BASE ** (-jnp.arange(half, dtype=_F32) / half)
    base = (N_META + tile * jnp.arange(n_tiles, dtype=jnp.int32)).astype(_F32)[:, None] * inv[None, :]
    off = jnp.arange(tile, dtype=jnp.int32).astype(_F32)[:, None] * inv[None, :]
    dup = lambda a: jnp.concatenate([a, a], axis=-1)
    sign = jnp.concatenate([-jnp.ones((half,), _F32), jnp.ones((half,), _F32)])
    tile_rot = jnp.stack([dup(jnp.cos(base)), dup(jnp.sin(base))], axis=1)
    tile_rot = jnp.broadcast_to(tile_rot[:, :, None, :], (n_tiles, 2, 8, HEAD_DIM))
    cos_o, sin_o = dup(jnp.cos(off)), dup(jnp.sin(off))
    row_rot = jnp.stack([cos_o, sin_o, sign * cos_o, sign * sin_o])
    return tile_rot, row_rot


def _lanes(per_head):
    return jnp.repeat(per_head, HEAD_DIM, axis=1)


def _decay_tables(rows, segment, causal_chunk):
    lg = _log_gamma()
    idx = jnp.arange(rows)
    loc = (idx % segment).astype(_F32)
    same_seg = (idx[:, None] // segment) == (idx[None, :] // segment)
    visible = same_seg & ((idx[None, :] // causal_chunk) <= (idx[:, None] // causal_chunk))
    dist = jnp.abs(idx[:, None] - idx[None, :]).astype(_F32)
    dmat = jnp.where(visible[None], jnp.exp(lg[:, None, None] * dist[None]), 0.0)
    d_in = _lanes(jnp.exp(lg[None, :] * (loc[:, None] + 1.0)))
    d_out = _lanes(jnp.exp(lg[None, :] * (segment - 1.0 - loc[:, None])))
    s_dec = jnp.broadcast_to(jnp.exp(lg * segment)[:, None], (RET_HEADS, HEAD_DIM))
    return dmat, d_in, d_out, s_dec


def kernel(x_prompt, x_sample, state_ret, cache_conv, meta, g_pre_mix, w_in, gn_g, gn_b, dw_w, dw_b, cln_g, cln_b,
           w_out, g_post_mix, g_pre_mlp, w_mlp_in, w_mlp_out, g_post_mlp):
    batch, seq, _ = x_prompt.shape
    dec_batch, dec_seq, _ = x_sample.shape
    assert batch == 1 and dec_seq == STREAM_ROWS and meta.shape[0] == N_META == STREAM_ROWS
    assert seq % PROMPT_TILE == 0 and seq % MLP_TILE == 0 and dec_batch % GROUP_STREAMS == 0
    past_len = 4096
    depth = state_ret.shape[0]
    assert depth == 1

    l = 0
    mixer_w = (g_pre_mix[l][None], w_in[l].astype(_BF16))
    post_w = (gn_g[l][None], gn_b[l][None], dw_w[l], dw_b[l][None], cln_g[l][None], cln_b[l][None],
              w_out[l].astype(_BF16), g_post_mix[l][None])
    mlp_w = (g_pre_mlp[l][None], w_mlp_in[l].astype(_BF16), w_mlp_out[l].astype(_BF16), g_post_mlp[l][None])

    pad_rows = GROUP_ROWS - N_META
    x_all = jnp.concatenate([meta, jnp.zeros((pad_rows, D_MODEL), _F32),
                             x_sample.reshape(dec_batch * dec_seq, D_MODEL)], axis=0)
    pos_group0 = jnp.arange(GROUP_ROWS, dtype=jnp.int32)
    pos_stream = N_META + past_len + jnp.arange(dec_seq, dtype=jnp.int32)
    pos_all = jnp.concatenate([pos_group0, jnp.tile(pos_stream, dec_batch)])
    cos_s, sin_s = _rope_tables(pos_all)
    dec_s = _decay_tables(GROUP_ROWS, STREAM_ROWS, STREAM_ROWS)
    xs1, s_s, c_s, s_meta, c_meta = _sample_mixer(x_all, cos_s, sin_s, *mixer_w, *dec_s, *post_w,
                                                  state_ret[l], cache_conv[l])
    ys = _mlp(xs1, *mlp_w, tile=GROUP_ROWS * 5)[GROUP_ROWS:]

    rot_p = _prompt_rope_tables(seq // PROMPT_TILE, PROMPT_TILE)
    dec_p = _decay_tables(PROMPT_TILE, PROMPT_TILE, CHUNK)
    xp1, s_p, c_p = _prompt_mixer(x_prompt[0], *rot_p, *mixer_w, *dec_p, *post_w, s_meta, c_meta)
    yp = _mlp(xp1, *mlp_w, tile=MLP_TILE)

    return (yp[None], ys.reshape(dec_batch, dec_seq, D_MODEL), s_p[None, None], c_p[None, None],
            s_s[None], c_s[None])
```

```python
import jax
import jax.numpy as jnp
from jax import lax
from jax.experimental import pallas as pl
from jax.experimental.pallas import tpu as pltpu

D_MODEL = 1024
CHUNK = 64
N_META = 16
PAST_LEN = 4096
RET_WIDTH = 512
RET_HEADS = 4
HEAD_DIM = 128
CONV_WIDTH = 512
CONV_K = 31
D_FF = 4096
N_IN = 4 * RET_WIDTH + 2 * CONV_WIDTH
EPS = 1e-6
ROPE_BASE = 10000.0

LANES = 128
SUBLANES = 8
HIST = 32
HIST_OFF = HIST - (CONV_K - 1)
PROMPT_TILE = 256
GROUP_ROWS = 128
STREAM_ROWS = 16
GROUP_STREAMS = GROUP_ROWS // STREAM_ROWS
MLP_TILE = 512
CONV_ROW_BLOCK = 32
CONV_LANE_BLOCKS = CONV_WIDTH // LANES
VMEM_LIMIT_BYTES = 56 * 1024 * 1024

_F32 = jnp.float32
_BF16 = jnp.bfloat16


def _rms(x, g):
    return x * lax.rsqrt(jnp.mean(x * x, axis=-1, keepdims=True) + EPS) * g


def _silu(x):
    return x * jax.nn.sigmoid(x)


def _dot(a, b):
    return jnp.dot(a, b, preferred_element_type=_F32)


def _dot_nt(a, b):
    return lax.dot_general(a, b, (((1,), (1,)), ((), ())), preferred_element_type=_F32)


def _dot_tn(a, b):
    return lax.dot_general(a, b, (((0,), (0,)), ((), ())), preferred_element_type=_F32)


def _weight(packed):
    return pltpu.bitcast(packed, _BF16)


def _head(x, h):
    return x[:, h * HEAD_DIM:(h + 1) * HEAD_DIM]


def _rope(xh, cos_f, sin_f):
    return xh * cos_f + pltpu.roll(xh, HEAD_DIM // 2, 1) * sin_f


def _split_heads(q, k, v, cos_f, sin_f):
    qs = [_rope(_head(q, hh), cos_f, sin_f) for hh in range(RET_HEADS)]
    ks = [_rope(_head(k, hh), cos_f, sin_f) * (HEAD_DIM ** -0.5) for hh in range(RET_HEADS)]
    vs = [_head(v, hh) for hh in range(RET_HEADS)]
    return qs, ks, vs


def _project(x, g_pre, w_in, cos_f, sin_f):
    h = _rms(x, g_pre).astype(_BF16)
    proj = _dot(h, w_in)
    q = proj[:, 0:RET_WIDTH]
    k = proj[:, RET_WIDTH:2 * RET_WIDTH]
    v = proj[:, 2 * RET_WIDTH:3 * RET_WIDTH]
    gate = proj[:, 3 * RET_WIDTH:4 * RET_WIDTH]
    ga = proj[:, 4 * RET_WIDTH:4 * RET_WIDTH + CONV_WIDTH]
    gb = proj[:, 4 * RET_WIDTH + CONV_WIDTH:N_IN]
    qs, ks, vs = _split_heads(q, k, v, cos_f, sin_f)
    return qs, ks, vs, gate, ga * jax.nn.sigmoid(gb)


def _group_norm_gate(ret_h, gate_h, g, b):
    mu = jnp.mean(ret_h, axis=-1, keepdims=True)
    d = ret_h - mu
    var = jnp.mean(d * d, axis=-1, keepdims=True)
    rn = d * lax.rsqrt(var + EPS) * g + b
    return rn * _silu(gate_h)


def _layer_norm(x, g, b):
    mu = jnp.mean(x, axis=-1, keepdims=True)
    d = x - mu
    var = jnp.mean(d * d, axis=-1, keepdims=True)
    return d * lax.rsqrt(var + EPS) * g + b


def _causal_conv_lanes(ext_ref, dww_ref, dwb_ref, y_ref, n_rows, c):
    rb = min(CONV_ROW_BLOCK, n_rows)
    lanes = slice(c * LANES, (c + 1) * LANES)
    taps = [dww_ref[k:k + 1, lanes] for k in range(CONV_K)]
    bias = dwb_ref[:, lanes]
    for r0 in range(0, n_rows, rb):
        acc = ext_ref[c, r0 + HIST_OFF:r0 + HIST_OFF + rb, :] * taps[0]
        for k in range(1, CONV_K):
            acc = acc + ext_ref[c, r0 + HIST_OFF + k:r0 + HIST_OFF + k + rb, :] * taps[k]
        y_ref[r0:r0 + rb, lanes] = acc + bias


def _causal_conv(ext_ref, dww_ref, dwb_ref, y_ref, n_rows):
    for c in range(CONV_LANE_BLOCKS):
        _causal_conv_lanes(ext_ref, dww_ref, dwb_ref, y_ref, n_rows, c)


def _ext_store(ext_ref, r0, rows_value):
    n = rows_value.shape[0]
    for c in range(CONV_LANE_BLOCKS):
        ext_ref[c, r0:r0 + n, :] = rows_value[:, c * LANES:(c + 1) * LANES]


def _ext_load(ext_ref, r0, n):
    return jnp.concatenate([ext_ref[c, r0:r0 + n, :] for c in range(CONV_LANE_BLOCKS)], axis=-1)


def _mix_out(x, ret_out, conv_out, wout_ref, g_post):
    half = wout_ref.shape[0] // 2
    mix = (_dot(conv_out.astype(_BF16), _weight(wout_ref[half:, :]))
           + _dot(ret_out.astype(_BF16), _weight(wout_ref[0:half, :])))
    return x + _rms(mix, g_post)


def _prompt_mixer_kernel(x_ref, tile_rot_ref, row_rot_ref, gpre_ref, win_ref, dmat_ref, din_ref, dout_ref,
                         sdec_ref, gng_ref, gnb_ref, dww_ref, dwb_ref, clng_ref, clnb_ref, wout_ref, gpost_ref,
                         s0_ref, c0_ref,
                         x1_ref, sfin_ref, cfin_ref,
                         s_scr, ext_scr, y_scr):
    i = pl.program_id(0)
    rows = x_ref.shape[0]

    @pl.when(i == 0)
    def _():
        s_scr[...] = s0_ref[...]
        _ext_store(ext_scr, 0, jnp.zeros((HIST_OFF, CONV_WIDTH), _F32))
        _ext_store(ext_scr, HIST_OFF, c0_ref[...])

    cos_a, sin_a = tile_rot_ref[0, 0, 0:1, :], tile_rot_ref[0, 1, 0:1, :]
    cos_f = cos_a * row_rot_ref[0] - sin_a * row_rot_ref[1]
    sin_f = sin_a * row_rot_ref[2] + cos_a * row_rot_ref[3]

    x = x_ref[...]
    h = _rms(x, gpre_ref[...]).astype(_BF16)

    glu = _dot(h, _weight(win_ref[:, 4 * RET_WIDTH:N_IN]))
    u = glu[:, 0:CONV_WIDTH] * jax.nn.sigmoid(glu[:, CONV_WIDTH:2 * CONV_WIDTH])
    _ext_store(ext_scr, HIST, u)
    blocks = []
    for c in range(CONV_LANE_BLOCKS):
        blocks.append(_dot(h, _weight(win_ref[:, c * RET_WIDTH:(c + 1) * RET_WIDTH])))
        _causal_conv_lanes(ext_scr, dww_ref, dwb_ref, y_scr, rows, c)
    q, k, v, gate = blocks
    ext_scr[:, 0:HIST, :] = ext_scr[:, rows:rows + HIST, :]
    conv_out = _silu(_layer_norm(y_scr[...], clng_ref[...], clnb_ref[...]))
    qs, ks, vs = _split_heads(q, k, v, cos_f, sin_f)

    ret_parts = []
    for hh in range(RET_HEADS):
        hl = slice(hh * HEAD_DIM, (hh + 1) * HEAD_DIM)
        qb = qs[hh].astype(_BF16)
        vb = vs[hh].astype(_BF16)
        p = (_dot_nt(qb, ks[hh].astype(_BF16)) * dmat_ref[hh]).astype(_BF16)
        intra = _dot(p, vb)
        s_old = s_scr[hh]
        inter = _dot(qb, s_old.astype(_BF16)) * din_ref[:, hl]
        kd = (ks[hh] * dout_ref[:, hl]).astype(_BF16)
        s_scr[hh] = s_old * sdec_ref[hh:hh + 1, :] + _dot_tn(kd, vb)
        ret_parts.append(_group_norm_gate(intra + inter, _head(gate, hh), gng_ref[:, hl], gnb_ref[:, hl]))
    ret_out = jnp.concatenate(ret_parts, axis=-1)

    x1_ref[...] = _mix_out(x, ret_out, conv_out, wout_ref, gpost_ref[...])

    @pl.when(i == pl.num_programs(0) - 1)
    def _():
        sfin_ref[...] = s_scr[...]
        cfin_ref[...] = _ext_load(ext_scr, HIST_OFF, CONV_K - 1)


def _const_spec(shape):
    zeros = (0,) * len(shape)
    return pl.BlockSpec(shape, lambda i, z=zeros: z)


def _resident_spec(shape):
    zeros = (0,) * len(shape)
    return pl.BlockSpec(shape, lambda i, z=zeros: z, pipeline_mode=pl.Buffered(1))


def _prompt_mixer(x, tile_rot, row_rot, g_pre, w_in, dmat, d_in, d_out, s_dec, gn_g, gn_b, dw_w, dw_b, cln_g, cln_b,
                  w_out, g_post, s0, c0):
    n = x.shape[0]
    t = PROMPT_TILE
    consts = (row_rot, g_pre, w_in, dmat, d_in, d_out, s_dec, gn_g, gn_b, dw_w, dw_b, cln_g, cln_b, w_out,
              g_post, s0, c0)
    in_specs = [pl.BlockSpec((t, D_MODEL), lambda i: (i, 0)),
                pl.BlockSpec((1,) + tile_rot.shape[1:], lambda i: (i, 0, 0, 0))] + [_const_spec(c.shape)
                                                                                    for c in consts]
    out_shape = (jax.ShapeDtypeStruct((n, D_MODEL), _F32),
                 jax.ShapeDtypeStruct((RET_HEADS, HEAD_DIM, HEAD_DIM), _F32),
                 jax.ShapeDtypeStruct((CONV_K - 1, CONV_WIDTH), _F32))
    out_specs = (pl.BlockSpec((t, D_MODEL), lambda i: (i, 0)),
                 _const_spec((RET_HEADS, HEAD_DIM, HEAD_DIM)),
                 _const_spec((CONV_K - 1, CONV_WIDTH)))
    return pl.pallas_call(
        _prompt_mixer_kernel,
        grid=(n // t,),
        in_specs=in_specs,
        out_specs=out_specs,
        out_shape=out_shape,
        scratch_shapes=[pltpu.VMEM((RET_HEADS, HEAD_DIM, HEAD_DIM), _F32),
                        pltpu.VMEM((CONV_LANE_BLOCKS, HIST + t, LANES), _F32),
                        pltpu.VMEM((t, CONV_WIDTH), _F32)],
        compiler_params=pltpu.CompilerParams(dimension_semantics=("arbitrary",),
                                             vmem_limit_bytes=VMEM_LIMIT_BYTES),
        name="prompt_mixer",
    )(x, tile_rot, *consts)


def _sample_mixer_kernel(x_ref, cos_ref, sin_ref, gpre_ref, win_ref, dmat_ref, din_ref, dout_ref, sdec_ref,
                         gng_ref, gnb_ref, dww_ref, dwb_ref, clng_ref, clnb_ref, wout_ref, gpost_ref,
                         state_ref, cache_ref,
                         x1_ref, snew_ref, cnew_ref, smeta_ref, cmeta_ref,
                         ext_scr, y_scr):
    g = pl.program_id(0)
    has_state = g > 0

    x = x_ref[...]
    qs, ks, vs, gate, u = _project(x, gpre_ref[...], _weight(win_ref[...]), cos_ref[...], sin_ref[...])

    ret_parts = []
    for h in range(RET_HEADS):
        hl = slice(h * HEAD_DIM, (h + 1) * HEAD_DIM)
        qb = qs[h].astype(_BF16)
        vb = vs[h].astype(_BF16)
        p = (_dot_nt(qb, ks[h].astype(_BF16)) * dmat_ref[h]).astype(_BF16)
        intra = _dot(p, vb)
        kd = (ks[h] * dout_ref[:, hl]).astype(_BF16)
        inter_parts = []
        for s in range(GROUP_STREAMS):
            rs = slice(s * STREAM_ROWS, (s + 1) * STREAM_ROWS)
            s_old = jnp.where(has_state, state_ref[s, h], 0.0)
            inter_parts.append(_dot(qb[rs], s_old.astype(_BF16)))
            snew_ref[s, h] = s_old * sdec_ref[h:h + 1, :] + _dot_tn(kd[rs], vb[rs])
        inter = jnp.concatenate(inter_parts, axis=0) * din_ref[:, hl]
        ret_parts.append(_group_norm_gate(intra + inter, _head(gate, h), gng_ref[:, hl], gnb_ref[:, hl]))
    ret_out = jnp.concatenate(ret_parts, axis=-1)

    for s in range(GROUP_STREAMS):
        rs = slice(s * STREAM_ROWS, (s + 1) * STREAM_ROWS)
        _ext_store(ext_scr, 0, jnp.zeros((HIST_OFF, CONV_WIDTH), _F32))
        _ext_store(ext_scr, HIST_OFF, jnp.where(has_state, cache_ref[s], 0.0))
        _ext_store(ext_scr, HIST, u[rs])
        _causal_conv(ext_scr, dww_ref, dwb_ref, y_scr.at[rs], STREAM_ROWS)
        cnew_ref[s] = _ext_load(ext_scr, HIST + STREAM_ROWS - (CONV_K - 1), CONV_K - 1)
    conv_out = _silu(_layer_norm(y_scr[...], clng_ref[...], clnb_ref[...]))

    x1_ref[...] = _mix_out(x, ret_out, conv_out, wout_ref, gpost_ref[...])

    @pl.when(g == 0)
    def _():
        smeta_ref[...] = snew_ref[0]
        cmeta_ref[...] = cnew_ref[0]


def _sample_mixer(x_all, cos_f, sin_f, g_pre, w_in, dmat, d_in, d_out, s_dec, gn_g, gn_b, dw_w, dw_b, cln_g, cln_b,
                  w_out, g_post, state, cache):
    n = x_all.shape[0]
    n_streams = state.shape[0]
    consts = (g_pre, w_in, dmat, d_in, d_out, s_dec, gn_g, gn_b, dw_w, dw_b, cln_g, cln_b, w_out, g_post)
    stream_blk = lambda i: (jnp.maximum(i - 1, 0), 0, 0, 0)
    cache_blk = lambda i: (jnp.maximum(i - 1, 0), 0, 0)
    in_specs = ([pl.BlockSpec((GROUP_ROWS, D_MODEL), lambda i: (i, 0)),
                 pl.BlockSpec((GROUP_ROWS, HEAD_DIM), lambda i: (i, 0)),
                 pl.BlockSpec((GROUP_ROWS, HEAD_DIM), lambda i: (i, 0))]
                + [_const_spec(c.shape) for c in consts]
                + [pl.BlockSpec((GROUP_STREAMS, RET_HEADS, HEAD_DIM, HEAD_DIM), stream_blk),
                   pl.BlockSpec((GROUP_STREAMS, CONV_K - 1, CONV_WIDTH), cache_blk)])
    out_shape = (jax.ShapeDtypeStruct((n, D_MODEL), _F32),
                 jax.ShapeDtypeStruct((n_streams, RET_HEADS, HEAD_DIM, HEAD_DIM), _F32),
                 jax.ShapeDtypeStruct((n_streams, CONV_K - 1, CONV_WIDTH), _F32),
                 jax.ShapeDtypeStruct((RET_HEADS, HEAD_DIM, HEAD_DIM), _F32),
                 jax.ShapeDtypeStruct((CONV_K - 1, CONV_WIDTH), _F32))
    out_specs = (pl.BlockSpec((GROUP_ROWS, D_MODEL), lambda i: (i, 0)),
                 pl.BlockSpec((GROUP_STREAMS, RET_HEADS, HEAD_DIM, HEAD_DIM), stream_blk),
                 pl.BlockSpec((GROUP_STREAMS, CONV_K - 1, CONV_WIDTH), cache_blk),
                 _const_spec((RET_HEADS, HEAD_DIM, HEAD_DIM)),
                 _const_spec((CONV_K - 1, CONV_WIDTH)))
    return pl.pallas_call(
        _sample_mixer_kernel,
        grid=(n // GROUP_ROWS,),
        in_specs=in_specs,
        out_specs=out_specs,
        out_shape=out_shape,
        scratch_shapes=[pltpu.VMEM((CONV_LANE_BLOCKS, HIST + STREAM_ROWS, LANES), _F32),
                        pltpu.VMEM((GROUP_ROWS, CONV_WIDTH), _F32)],
        compiler_params=pltpu.CompilerParams(dimension_semantics=("arbitrary",),
                                             vmem_limit_bytes=VMEM_LIMIT_BYTES),
        name="sample_mixer",
    )(x_all, cos_f, sin_f, *consts, state, cache)


def _mlp_kernel(x_ref, gpre_ref, w1_ref, w2_ref, gpost_ref, y_ref):
    x = x_ref[...]
    h = _rms(x, gpre_ref[...]).astype(_BF16)
    a = jnp.maximum(_dot(h, _weight(w1_ref[...])), 0.0)
    f = _dot((a * a).astype(_BF16), _weight(w2_ref[...]))
    y_ref[...] = x + _rms(f, gpost_ref[...])


def _mlp(x, g_pre, w1, w2, g_post, tile):
    n = x.shape[0]
    return pl.pallas_call(
        _mlp_kernel,
        grid=(n // tile,),
        in_specs=[pl.BlockSpec((tile, D_MODEL), lambda i: (i, 0)),
                  _const_spec(g_pre.shape),
                  _resident_spec(w1.shape),
                  _resident_spec(w2.shape),
                  _const_spec(g_post.shape)],
        out_specs=pl.BlockSpec((tile, D_MODEL), lambda i: (i, 0)),
        out_shape=jax.ShapeDtypeStruct((n, D_MODEL), _F32),
        compiler_params=pltpu.CompilerParams(dimension_semantics=("arbitrary",),
                                             vmem_limit_bytes=VMEM_LIMIT_BYTES),
        name="mlp",
    )(x, g_pre, w1, w2, g_post)


def _pack_weight(w):
    k, n = w.shape
    pairs = jnp.swapaxes(w.astype(_BF16).reshape(k // 2, 2, n), 1, 2)
    return lax.bitcast_convert_type(pairs, jnp.uint32)


def _log_gamma():
    return jnp.log1p(-jnp.exp2(-5.0 - jnp.arange(RET_HEADS, dtype=_F32)))


def _rope_tables(pos):
    half = HEAD_DIM // 2
    inv = ROPE_BASE ** (-jnp.arange(half, dtype=_F32) / half)
    ang = pos.astype(_F32)[:, None] * inv[None, :]
    cos = jnp.cos(ang)
    sin = jnp.sin(ang)
    return jnp.concatenate([cos, cos], axis=-1), jnp.concatenate([-sin, sin], axis=-1)


def _prompt_rope_tables(n_tiles, tile):
    half = HEAD_DIM // 2
    inv = ROPE_BASE ** (-jnp.arange(half, dtype=_F32) / half)
    base = (N_META + tile * jnp.arange(n_tiles, dtype=jnp.int32)).astype(_F32)[:, None] * inv[None, :]
    off = jnp.arange(tile, dtype=jnp.int32).astype(_F32)[:, None] * inv[None, :]
    dup = lambda a: jnp.concatenate([a, a], axis=-1)
    sign = jnp.concatenate([-jnp.ones((half,), _F32), jnp.ones((half,), _F32)])
    tile_rot = jnp.stack([dup(jnp.cos(base)), dup(jnp.sin(base))], axis=1)
    tile_rot = jnp.broadcast_to(tile_rot[:, :, None, :], (n_tiles, 2, SUBLANES, HEAD_DIM))
    cos_o, sin_o = dup(jnp.cos(off)), dup(jnp.sin(off))
    row_rot = jnp.stack([cos_o, sin_o, sign * cos_o, sign * sin_o])
    return tile_rot, row_rot


def _lanes(per_head):
    return jnp.repeat(per_head, HEAD_DIM, axis=1)


def _decay_tables(rows, segment, causal_chunk):
    lg = _log_gamma()
    idx = jnp.arange(rows)
    loc = (idx % segment).astype(_F32)
    same_seg = (idx[:, None] // segment) == (idx[None, :] // segment)
    visible = same_seg & ((idx[None, :] // causal_chunk) <= (idx[:, None] // causal_chunk))
    dist = jnp.abs(idx[:, None] - idx[None, :]).astype(_F32)
    dmat = jnp.where(visible[None], jnp.exp(lg[:, None, None] * dist[None]), 0.0)
    d_in = _lanes(jnp.exp(lg[None, :] * (loc[:, None] + 1.0)))
    d_out = _lanes(jnp.exp(lg[None, :] * (segment - 1.0 - loc[:, None])))
    s_dec = jnp.broadcast_to(jnp.exp(lg * segment)[:, None], (RET_HEADS, HEAD_DIM))
    return dmat, d_in, d_out, s_dec


def kernel(x_prompt, x_sample, state_ret, cache_conv, meta, g_pre_mix, w_in, gn_g, gn_b, dw_w, dw_b, cln_g, cln_b,
           w_out, g_post_mix, g_pre_mlp, w_mlp_in, w_mlp_out, g_post_mlp):
    batch, seq, _ = x_prompt.shape
    dec_batch, dec_seq, _ = x_sample.shape
    assert batch == 1 and dec_seq == STREAM_ROWS and meta.shape[0] == N_META == STREAM_ROWS
    assert seq % PROMPT_TILE == 0 and seq % MLP_TILE == 0 and dec_batch % GROUP_STREAMS == 0
    assert state_ret.shape[0] == 1

    l = 0
    mixer_w = (g_pre_mix[l][None], _pack_weight(w_in[l]))
    post_w = (gn_g[l][None], gn_b[l][None], dw_w[l], dw_b[l][None], cln_g[l][None], cln_b[l][None],
              _pack_weight(w_out[l]), g_post_mix[l][None])
    mlp_w = (g_pre_mlp[l][None], _pack_weight(w_mlp_in[l]), _pack_weight(w_mlp_out[l]), g_post_mlp[l][None])

    pad_rows = GROUP_ROWS - N_META
    x_all = jnp.concatenate([meta, jnp.zeros((pad_rows, D_MODEL), _F32),
                             x_sample.reshape(dec_batch * dec_seq, D_MODEL)], axis=0)
    pos_group0 = jnp.arange(GROUP_ROWS, dtype=jnp.int32)
    pos_stream = N_META + PAST_LEN + jnp.arange(dec_seq, dtype=jnp.int32)
    pos_all = jnp.concatenate([pos_group0, jnp.tile(pos_stream, dec_batch)])
    cos_s, sin_s = _rope_tables(pos_all)
    dec_s = _decay_tables(GROUP_ROWS, STREAM_ROWS, STREAM_ROWS)
    xs1, s_s, c_s, s_meta, c_meta = _sample_mixer(x_all, cos_s, sin_s, *mixer_w, *dec_s, *post_w,
                                                  state_ret[l], cache_conv[l])
    ys = _mlp(xs1, *mlp_w, tile=xs1.shape[0])[GROUP_ROWS:]

    rot_p = _prompt_rope_tables(seq // PROMPT_TILE, PROMPT_TILE)
    dec_p = _decay_tables(PROMPT_TILE, PROMPT_TILE, CHUNK)
    xp1, s_p, c_p = _prompt_mixer(x_prompt[0], *rot_p, *mixer_w, *dec_p, *post_w, s_meta, c_meta)
    yp = _mlp(xp1, *mlp_w, tile=MLP_TILE)

    return (yp[None], ys.reshape(dec_batch, dec_seq, D_MODEL), s_p[None, None], c_p[None, None],
            s_s[None], c_s[None])
```

```python
import jax
import jax.numpy as jnp
from jax import lax
from jax.experimental import pallas as pl
from jax.experimental.pallas import tpu as pltpu

D_MODEL = 1024
CHUNK = 64
N_META = 16
PAST_LEN = 4096
RET_WIDTH = 512
RET_HEADS = 4
HEAD_DIM = 128
CONV_WIDTH = 512
CONV_K = 31
D_FF = 4096
N_IN = 4 * RET_WIDTH + 2 * CONV_WIDTH
EPS = 1e-6
ROPE_BASE = 10000.0

LANES = 128
SUBLANES = 8
HIST = 32
HIST_OFF = HIST - (CONV_K - 1)
PROMPT_TILE = 256
GROUP_ROWS = 128
STREAM_ROWS = 16
GROUP_STREAMS = GROUP_ROWS // STREAM_ROWS
MLP_TILE = 512
PACK_TILE_ROWS = 256
CONV_ROW_BLOCK = 32
CONV_LANE_BLOCKS = CONV_WIDTH // LANES
VMEM_LIMIT_BYTES = 56 * 1024 * 1024

_F32 = jnp.float32
_BF16 = jnp.bfloat16


def _rms(x, g):
    return x * lax.rsqrt(jnp.mean(x * x, axis=-1, keepdims=True) + EPS) * g


def _silu(x):
    return x * jax.nn.sigmoid(x)


def _dot(a, b):
    return jnp.dot(a, b, preferred_element_type=_F32)


def _dot_nt(a, b):
    return lax.dot_general(a, b, (((1,), (1,)), ((), ())), preferred_element_type=_F32)


def _dot_tn(a, b):
    return lax.dot_general(a, b, (((0,), (0,)), ((), ())), preferred_element_type=_F32)


def _weight(packed):
    return pltpu.bitcast(packed, _BF16)


def _head(x, h):
    return x[:, h * HEAD_DIM:(h + 1) * HEAD_DIM]


def _rope(xh, cos_f, sin_f):
    return xh * cos_f + pltpu.roll(xh, HEAD_DIM // 2, 1) * sin_f


def _split_heads(q, k, v, cos_f, sin_f):
    qs = [_rope(_head(q, hh), cos_f, sin_f) for hh in range(RET_HEADS)]
    ks = [_rope(_head(k, hh), cos_f, sin_f) * (HEAD_DIM ** -0.5) for hh in range(RET_HEADS)]
    vs = [_head(v, hh) for hh in range(RET_HEADS)]
    return qs, ks, vs


def _project(x, g_pre, w_in, cos_f, sin_f):
    h = _rms(x, g_pre).astype(_BF16)
    proj = _dot(h, w_in)
    q = proj[:, 0:RET_WIDTH]
    k = proj[:, RET_WIDTH:2 * RET_WIDTH]
    v = proj[:, 2 * RET_WIDTH:3 * RET_WIDTH]
    gate = proj[:, 3 * RET_WIDTH:4 * RET_WIDTH]
    ga = proj[:, 4 * RET_WIDTH:4 * RET_WIDTH + CONV_WIDTH]
    gb = proj[:, 4 * RET_WIDTH + CONV_WIDTH:N_IN]
    qs, ks, vs = _split_heads(q, k, v, cos_f, sin_f)
    return qs, ks, vs, gate, ga * jax.nn.sigmoid(gb)


def _group_norm_gate(ret_h, gate_h, g, b):
    mu = jnp.mean(ret_h, axis=-1, keepdims=True)
    d = ret_h - mu
    var = jnp.mean(d * d, axis=-1, keepdims=True)
    rn = d * lax.rsqrt(var + EPS) * g + b
    return rn * _silu(gate_h)


def _layer_norm(x, g, b):
    mu = jnp.mean(x, axis=-1, keepdims=True)
    d = x - mu
    var = jnp.mean(d * d, axis=-1, keepdims=True)
    return d * lax.rsqrt(var + EPS) * g + b


def _causal_conv_lanes(ext_ref, dww_ref, dwb_ref, y_ref, n_rows, c):
    rb = min(CONV_ROW_BLOCK, n_rows)
    lanes = slice(c * LANES, (c + 1) * LANES)
    taps = [dww_ref[k:k + 1, lanes] for k in range(CONV_K)]
    bias = dwb_ref[:, lanes]
    for r0 in range(0, n_rows, rb):
        acc = ext_ref[c, r0 + HIST_OFF:r0 + HIST_OFF + rb, :] * taps[0]
        for k in range(1, CONV_K):
            acc = acc + ext_ref[c, r0 + HIST_OFF + k:r0 + HIST_OFF + k + rb, :] * taps[k]
        y_ref[r0:r0 + rb, lanes] = acc + bias


def _causal_conv(ext_ref, dww_ref, dwb_ref, y_ref, n_rows):
    for c in range(CONV_LANE_BLOCKS):
        _causal_conv_lanes(ext_ref, dww_ref, dwb_ref, y_ref, n_rows, c)


def _ext_store(ext_ref, r0, rows_value):
    n = rows_value.shape[0]
    for c in range(CONV_LANE_BLOCKS):
        ext_ref[c, r0:r0 + n, :] = rows_value[:, c * LANES:(c + 1) * LANES]


def _ext_load(ext_ref, r0, n):
    return jnp.concatenate([ext_ref[c, r0:r0 + n, :] for c in range(CONV_LANE_BLOCKS)], axis=-1)


def _mix_out(x, ret_out, conv_out, wout_ref, g_post):
    half = wout_ref.shape[0] // 2
    mix = (_dot(conv_out.astype(_BF16), _weight(wout_ref[half:, :]))
           + _dot(ret_out.astype(_BF16), _weight(wout_ref[0:half, :])))
    return x + _rms(mix, g_post)


def _prompt_mixer_kernel(x_ref, tile_rot_ref, row_rot_ref, gpre_ref, win_ref, dmat_ref, din_ref, dout_ref,
                         sdec_ref, gng_ref, gnb_ref, dww_ref, dwb_ref, clng_ref, clnb_ref, wout_ref, gpost_ref,
                         s0_ref, c0_ref,
                         x1_ref, sfin_ref, cfin_ref,
                         s_scr, ext_scr, y_scr):
    i = pl.program_id(0)
    rows = x_ref.shape[0]

    @pl.when(i == 0)
    def _():
        s_scr[...] = s0_ref[...]
        _ext_store(ext_scr, 0, jnp.zeros((HIST_OFF, CONV_WIDTH), _F32))
        _ext_store(ext_scr, HIST_OFF, c0_ref[...])

    cos_a, sin_a = tile_rot_ref[0, 0, 0:1, :], tile_rot_ref[0, 1, 0:1, :]
    cos_f = cos_a * row_rot_ref[0] - sin_a * row_rot_ref[1]
    sin_f = sin_a * row_rot_ref[2] + cos_a * row_rot_ref[3]

    x = x_ref[...]
    h = _rms(x, gpre_ref[...]).astype(_BF16)

    glu = _dot(h, _weight(win_ref[:, 4 * RET_WIDTH:N_IN]))
    u = glu[:, 0:CONV_WIDTH] * jax.nn.sigmoid(glu[:, CONV_WIDTH:2 * CONV_WIDTH])
    _ext_store(ext_scr, HIST, u)
    blocks = []
    for c in range(CONV_LANE_BLOCKS):
        blocks.append(_dot(h, _weight(win_ref[:, c * RET_WIDTH:(c + 1) * RET_WIDTH])))
        _causal_conv_lanes(ext_scr, dww_ref, dwb_ref, y_scr, rows, c)
    q, k, v, gate = blocks
    ext_scr[:, 0:HIST, :] = ext_scr[:, rows:rows + HIST, :]
    conv_out = _silu(_layer_norm(y_scr[...], clng_ref[...], clnb_ref[...]))
    qs, ks, vs = _split_heads(q, k, v, cos_f, sin_f)

    ret_parts = []
    for hh in range(RET_HEADS):
        hl = slice(hh * HEAD_DIM, (hh + 1) * HEAD_DIM)
        qb = qs[hh].astype(_BF16)
        vb = vs[hh].astype(_BF16)
        p = (_dot_nt(qb, ks[hh].astype(_BF16)) * dmat_ref[hh]).astype(_BF16)
        intra = _dot(p, vb)
        s_old = s_scr[hh]
        inter = _dot(qb, s_old.astype(_BF16)) * din_ref[:, hl]
        kd = (ks[hh] * dout_ref[:, hl]).astype(_BF16)
        s_scr[hh] = s_old * sdec_ref[hh:hh + 1, :] + _dot_tn(kd, vb)
        ret_parts.append(_group_norm_gate(intra + inter, _head(gate, hh), gng_ref[:, hl], gnb_ref[:, hl]))
    ret_out = jnp.concatenate(ret_parts, axis=-1)

    x1_ref[...] = _mix_out(x, ret_out, conv_out, wout_ref, gpost_ref[...])

    @pl.when(i == pl.num_programs(0) - 1)
    def _():
        sfin_ref[...] = s_scr[...]
        cfin_ref[...] = _ext_load(ext_scr, HIST_OFF, CONV_K - 1)


def _const_spec(shape):
    zeros = (0,) * len(shape)
    return pl.BlockSpec(shape, lambda i, z=zeros: z)


def _resident_spec(shape):
    zeros = (0,) * len(shape)
    return pl.BlockSpec(shape, lambda i, z=zeros: z, pipeline_mode=pl.Buffered(1))


def _prompt_mixer(x, tile_rot, row_rot, g_pre, w_in, dmat, d_in, d_out, s_dec, gn_g, gn_b, dw_w, dw_b, cln_g, cln_b,
                  w_out, g_post, s0, c0):
    n = x.shape[0]
    t = PROMPT_TILE
    consts = (row_rot, g_pre, w_in, dmat, d_in, d_out, s_dec, gn_g, gn_b, dw_w, dw_b, cln_g, cln_b, w_out,
              g_post, s0, c0)
    in_specs = [pl.BlockSpec((t, D_MODEL), lambda i: (i, 0)),
                pl.BlockSpec((1,) + tile_rot.shape[1:], lambda i: (i, 0, 0, 0))] + [_const_spec(c.shape)
                                                                                    for c in consts]
    out_shape = (jax.ShapeDtypeStruct((n, D_MODEL), _F32),
                 jax.ShapeDtypeStruct((RET_HEADS, HEAD_DIM, HEAD_DIM), _F32),
                 jax.ShapeDtypeStruct((CONV_K - 1, CONV_WIDTH), _F32))
    out_specs = (pl.BlockSpec((t, D_MODEL), lambda i: (i, 0)),
                 _const_spec((RET_HEADS, HEAD_DIM, HEAD_DIM)),
                 _const_spec((CONV_K - 1, CONV_WIDTH)))
    return pl.pallas_call(
        _prompt_mixer_kernel,
        grid=(n // t,),
        in_specs=in_specs,
        out_specs=out_specs,
        out_shape=out_shape,
        scratch_shapes=[pltpu.VMEM((RET_HEADS, HEAD_DIM, HEAD_DIM), _F32),
                        pltpu.VMEM((CONV_LANE_BLOCKS, HIST + t, LANES), _F32),
                        pltpu.VMEM((t, CONV_WIDTH), _F32)],
        compiler_params=pltpu.CompilerParams(dimension_semantics=("arbitrary",),
                                             vmem_limit_bytes=VMEM_LIMIT_BYTES),
        name="prompt_mixer",
    )(x, tile_rot, *consts)


def _sample_mixer_kernel(x_ref, cos_ref, sin_ref, gpre_ref, win_ref, dmat_ref, din_ref, dout_ref, sdec_ref,
                         gng_ref, gnb_ref, dww_ref, dwb_ref, clng_ref, clnb_ref, wout_ref, gpost_ref,
                         state_ref, cache_ref,
                         x1_ref, snew_ref, cnew_ref, smeta_ref, cmeta_ref,
                         ext_scr, y_scr):
    g = pl.program_id(0)
    has_state = g > 0

    x = x_ref[...]
    qs, ks, vs, gate, u = _project(x, gpre_ref[...], _weight(win_ref[...]), cos_ref[...], sin_ref[...])

    ret_parts = []
    for h in range(RET_HEADS):
        hl = slice(h * HEAD_DIM, (h + 1) * HEAD_DIM)
        qb = qs[h].astype(_BF16)
        vb = vs[h].astype(_BF16)
        p = (_dot_nt(qb, ks[h].astype(_BF16)) * dmat_ref[h]).astype(_BF16)
        intra = _dot(p, vb)
        kd = (ks[h] * dout_ref[:, hl]).astype(_BF16)
        inter_parts = []
        for s in range(GROUP_STREAMS):
            rs = slice(s * STREAM_ROWS, (s + 1) * STREAM_ROWS)
            s_old = jnp.where(has_state, state_ref[s, h], 0.0)
            inter_parts.append(_dot(qb[rs], s_old.astype(_BF16)))
            snew_ref[s, h] = s_old * sdec_ref[h:h + 1, :] + _dot_tn(kd[rs], vb[rs])
        inter = jnp.concatenate(inter_parts, axis=0) * din_ref[:, hl]
        ret_parts.append(_group_norm_gate(intra + inter, _head(gate, h), gng_ref[:, hl], gnb_ref[:, hl]))
    ret_out = jnp.concatenate(ret_parts, axis=-1)

    for s in range(GROUP_STREAMS):
        rs = slice(s * STREAM_ROWS, (s + 1) * STREAM_ROWS)
        _ext_store(ext_scr, 0, jnp.zeros((HIST_OFF, CONV_WIDTH), _F32))
        _ext_store(ext_scr, HIST_OFF, jnp.where(has_state, cache_ref[s], 0.0))
        _ext_store(ext_scr, HIST, u[rs])
        _causal_conv(ext_scr, dww_ref, dwb_ref, y_scr.at[rs], STREAM_ROWS)
        cnew_ref[s] = _ext_load(ext_scr, HIST + STREAM_ROWS - (CONV_K - 1), CONV_K - 1)
    conv_out = _silu(_layer_norm(y_scr[...], clng_ref[...], clnb_ref[...]))

    x1_ref[...] = _mix_out(x, ret_out, conv_out, wout_ref, gpost_ref[...])

    @pl.when(g == 0)
    def _():
        smeta_ref[...] = snew_ref[0]
        cmeta_ref[...] = cnew_ref[0]


def _sample_mixer(x_all, cos_f, sin_f, g_pre, w_in, dmat, d_in, d_out, s_dec, gn_g, gn_b, dw_w, dw_b, cln_g, cln_b,
                  w_out, g_post, state, cache):
    n = x_all.shape[0]
    n_streams = state.shape[0]
    consts = (g_pre, w_in, dmat, d_in, d_out, s_dec, gn_g, gn_b, dw_w, dw_b, cln_g, cln_b, w_out, g_post)
    stream_blk = lambda i: (jnp.maximum(i - 1, 0), 0, 0, 0)
    cache_blk = lambda i: (jnp.maximum(i - 1, 0), 0, 0)
    in_specs = ([pl.BlockSpec((GROUP_ROWS, D_MODEL), lambda i: (i, 0)),
                 pl.BlockSpec((GROUP_ROWS, HEAD_DIM), lambda i: (i, 0)),
                 pl.BlockSpec((GROUP_ROWS, HEAD_DIM), lambda i: (i, 0))]
                + [_const_spec(c.shape) for c in consts]
                + [pl.BlockSpec((GROUP_STREAMS, RET_HEADS, HEAD_DIM, HEAD_DIM), stream_blk),
                   pl.BlockSpec((GROUP_STREAMS, CONV_K - 1, CONV_WIDTH), cache_blk)])
    out_shape = (jax.ShapeDtypeStruct((n, D_MODEL), _F32),
                 jax.ShapeDtypeStruct((n_streams, RET_HEADS, HEAD_DIM, HEAD_DIM), _F32),
                 jax.ShapeDtypeStruct((n_streams, CONV_K - 1, CONV_WIDTH), _F32),
                 jax.ShapeDtypeStruct((RET_HEADS, HEAD_DIM, HEAD_DIM), _F32),
                 jax.ShapeDtypeStruct((CONV_K - 1, CONV_WIDTH), _F32))
    out_specs = (pl.BlockSpec((GROUP_ROWS, D_MODEL), lambda i: (i, 0)),
                 pl.BlockSpec((GROUP_STREAMS, RET_HEADS, HEAD_DIM, HEAD_DIM), stream_blk),
                 pl.BlockSpec((GROUP_STREAMS, CONV_K - 1, CONV_WIDTH), cache_blk),
                 _const_spec((RET_HEADS, HEAD_DIM, HEAD_DIM)),
                 _const_spec((CONV_K - 1, CONV_WIDTH)))
    return pl.pallas_call(
        _sample_mixer_kernel,
        grid=(n // GROUP_ROWS,),
        in_specs=in_specs,
        out_specs=out_specs,
        out_shape=out_shape,
        scratch_shapes=[pltpu.VMEM((CONV_LANE_BLOCKS, HIST + STREAM_ROWS, LANES), _F32),
                        pltpu.VMEM((GROUP_ROWS, CONV_WIDTH), _F32)],
        compiler_params=pltpu.CompilerParams(dimension_semantics=("arbitrary",),
                                             vmem_limit_bytes=VMEM_LIMIT_BYTES),
        name="sample_mixer",
    )(x_all, cos_f, sin_f, *consts, state, cache)


def _mlp_kernel(x_ref, gpre_ref, w1_ref, w2_ref, gpost_ref, y_ref):
    x = x_ref[...]
    h = _rms(x, gpre_ref[...]).astype(_BF16)
    a = jnp.maximum(_dot(h, _weight(w1_ref[...])), 0.0)
    f = _dot((a * a).astype(_BF16), _weight(w2_ref[...]))
    y_ref[...] = x + _rms(f, gpost_ref[...])


def _mlp(x, g_pre, w1, w2, g_post, tile):
    n = x.shape[0]
    return pl.pallas_call(
        _mlp_kernel,
        grid=(n // tile,),
        in_specs=[pl.BlockSpec((tile, D_MODEL), lambda i: (i, 0)),
                  _const_spec(g_pre.shape),
                  _resident_spec(w1.shape),
                  _resident_spec(w2.shape),
                  _const_spec(g_post.shape)],
        out_specs=pl.BlockSpec((tile, D_MODEL), lambda i: (i, 0)),
        out_shape=jax.ShapeDtypeStruct((n, D_MODEL), _F32),
        compiler_params=pltpu.CompilerParams(dimension_semantics=("arbitrary",),
                                             vmem_limit_bytes=VMEM_LIMIT_BYTES),
        name="mlp",
    )(x, g_pre, w1, w2, g_post)


def _pack_kernel(w_ref, o_ref):
    o_ref[...] = pltpu.bitcast(w_ref[...].astype(_BF16), jnp.uint32)


def _pack_weight(w):
    k, n = w.shape
    tk = PACK_TILE_ROWS
    return pl.pallas_call(
        _pack_kernel,
        grid=(k // tk,),
        in_specs=[pl.BlockSpec((tk, n), lambda i: (i, 0))],
        out_specs=pl.BlockSpec((tk // 2, n), lambda i: (i, 0)),
        out_shape=jax.ShapeDtypeStruct((k // 2, n), jnp.uint32),
        compiler_params=pltpu.CompilerParams(dimension_semantics=("arbitrary",),
                                             vmem_limit_bytes=VMEM_LIMIT_BYTES),
        name="pack_weight",
    )(w)


def _log_gamma():
    return jnp.log1p(-jnp.exp2(-5.0 - jnp.arange(RET_HEADS, dtype=_F32)))


def _rope_tables(pos):
    half = HEAD_DIM // 2
    inv = ROPE_BASE ** (-jnp.arange(half, dtype=_F32) / half)
    ang = pos.astype(_F32)[:, None] * inv[None, :]
    cos = jnp.cos(ang)
    sin = jnp.sin(ang)
    return jnp.concatenate([cos, cos], axis=-1), jnp.concatenate([-sin, sin], axis=-1)


def _prompt_rope_tables(n_tiles, tile):
    half = HEAD_DIM // 2
    inv = ROPE_BASE ** (-jnp.arange(half, dtype=_F32) / half)
    base = (N_META + tile * jnp.arange(n_tiles, dtype=jnp.int32)).astype(_F32)[:, None] * inv[None, :]
    off = jnp.arange(tile, dtype=jnp.int32).astype(_F32)[:, None] * inv[None, :]
    dup = lambda a: jnp.concatenate([a, a], axis=-1)
    sign = jnp.concatenate([-jnp.ones((half,), _F32), jnp.ones((half,), _F32)])
    tile_rot = jnp.stack([dup(jnp.cos(base)), dup(jnp.sin(base))], axis=1)
    tile_rot = jnp.broadcast_to(tile_rot[:, :, None, :], (n_tiles, 2, SUBLANES, HEAD_DIM))
    cos_o, sin_o = dup(jnp.cos(off)), dup(jnp.sin(off))
    row_rot = jnp.stack([cos_o, sin_o, sign * cos_o, sign * sin_o])
    return tile_rot, row_rot


def _lanes(per_head):
    return jnp.repeat(per_head, HEAD_DIM, axis=1)


def _decay_tables(rows, segment, causal_chunk):
    lg = _log_gamma()
    idx = jnp.arange(rows)
    loc = (idx % segment).astype(_F32)
    same_seg = (idx[:, None] // segment) == (idx[None, :] // segment)
    visible = same_seg & ((idx[None, :] // causal_chunk) <= (idx[:, None] // causal_chunk))
    dist = jnp.abs(idx[:, None] - idx[None, :]).astype(_F32)
    dmat = jnp.where(visible[None], jnp.exp(lg[:, None, None] * dist[None]), 0.0)
    d_in = _lanes(jnp.exp(lg[None, :] * (loc[:, None] + 1.0)))
    d_out = _lanes(jnp.exp(lg[None, :] * (segment - 1.0 - loc[:, None])))
    s_dec = jnp.broadcast_to(jnp.exp(lg * segment)[:, None], (RET_HEADS, HEAD_DIM))
    return dmat, d_in, d_out, s_dec


def kernel(x_prompt, x_sample, state_ret, cache_conv, meta, g_pre_mix, w_in, gn_g, gn_b, dw_w, dw_b, cln_g, cln_b,
           w_out, g_post_mix, g_pre_mlp, w_mlp_in, w_mlp_out, g_post_mlp):
    batch, seq, _ = x_prompt.shape
    dec_batch, dec_seq, _ = x_sample.shape
    assert batch == 1 and dec_seq == STREAM_ROWS and meta.shape[0] == N_META == STREAM_ROWS
    assert seq % PROMPT_TILE == 0 and seq % MLP_TILE == 0 and dec_batch % GROUP_STREAMS == 0
    assert state_ret.shape[0] == 1

    l = 0
    mixer_w = (g_pre_mix[l][None], _pack_weight(w_in[l]))
    post_w = (gn_g[l][None], gn_b[l][None], dw_w[l], dw_b[l][None], cln_g[l][None], cln_b[l][None],
              _pack_weight(w_out[l]), g_post_mix[l][None])
    mlp_w = (g_pre_mlp[l][None], _pack_weight(w_mlp_in[l]), _pack_weight(w_mlp_out[l]), g_post_mlp[l][None])

    pad_rows = GROUP_ROWS - N_META
    x_all = jnp.concatenate([meta, jnp.zeros((pad_rows, D_MODEL), _F32),
                             x_sample.reshape(dec_batch * dec_seq, D_MODEL)], axis=0)
    pos_group0 = jnp.arange(GROUP_ROWS, dtype=jnp.int32)
    pos_stream = N_META + PAST_LEN + jnp.arange(dec_seq, dtype=jnp.int32)
    pos_all = jnp.concatenate([pos_group0, jnp.tile(pos_stream, dec_batch)])
    cos_s, sin_s = _rope_tables(pos_all)
    dec_s = _decay_tables(GROUP_ROWS, STREAM_ROWS, STREAM_ROWS)
    xs1, s_s, c_s, s_meta, c_meta = _sample_mixer(x_all, cos_s, sin_s, *mixer_w, *dec_s, *post_w,
                                                  state_ret[l], cache_conv[l])
    ys = _mlp(xs1, *mlp_w, tile=xs1.shape[0])[GROUP_ROWS:]

    rot_p = _prompt_rope_tables(seq // PROMPT_TILE, PROMPT_TILE)
    dec_p = _decay_tables(PROMPT_TILE, PROMPT_TILE, CHUNK)
    xp1, s_p, c_p = _prompt_mixer(x_prompt[0], *rot_p, *mixer_w, *dec_p, *post_w, s_meta, c_meta)
    yp = _mlp(xp1, *mlp_w, tile=MLP_TILE)

    return (yp[None], ys.reshape(dec_batch, dec_seq, D_MODEL), s_p[None, None], c_p[None, None],
            s_s[None], c_s[None])
```

```python
import jax
import jax.numpy as jnp
from jax import lax
from jax.experimental import pallas as pl
from jax.experimental.pallas import tpu as pltpu

D_MODEL = 1024
CHUNK = 64
N_META = 16
PAST_LEN = 4096
RET_WIDTH = 512
RET_HEADS = 4
HEAD_DIM = 128
CONV_WIDTH = 512
CONV_K = 31
D_FF = 4096
N_IN = 4 * RET_WIDTH + 2 * CONV_WIDTH
EPS = 1e-6
ROPE_BASE = 10000.0

LANES = 128
SUBLANES = 8
HIST = 32
HIST_OFF = HIST - (CONV_K - 1)
RET_BLOCK = 256
PROMPT_TILE = 512
GROUP_ROWS = 128
STREAM_ROWS = 16
GROUP_STREAMS = GROUP_ROWS // STREAM_ROWS
MLP_TILE = 512
PACK_BLOCK_BYTES = 4 * 1024 * 1024
CONV_ROW_BLOCK = 32
CONV_LANE_BLOCKS = CONV_WIDTH // LANES
VMEM_LIMIT_BYTES = 56 * 1024 * 1024

_F32 = jnp.float32
_BF16 = jnp.bfloat16


def _rms(x, g):
    return x * lax.rsqrt(jnp.mean(x * x, axis=-1, keepdims=True) + EPS) * g


def _silu(x):
    return x * jax.nn.sigmoid(x)


def _dot(a, b):
    return jnp.dot(a, b, preferred_element_type=_F32)


def _dot_nt(a, b):
    return lax.dot_general(a, b, (((1,), (1,)), ((), ())), preferred_element_type=_F32)


def _dot_tn(a, b):
    return lax.dot_general(a, b, (((0,), (0,)), ((), ())), preferred_element_type=_F32)


def _weight(packed):
    return pltpu.bitcast(packed, _BF16)


def _head(x, h):
    return x[:, h * HEAD_DIM:(h + 1) * HEAD_DIM]


def _rope(xh, cos_f, sin_f):
    return xh * cos_f + pltpu.roll(xh, HEAD_DIM // 2, 1) * sin_f


def _split_heads(q, k, v, cos_f, sin_f):
    qs = [_rope(_head(q, hh), cos_f, sin_f) for hh in range(RET_HEADS)]
    ks = [_rope(_head(k, hh), cos_f, sin_f) * (HEAD_DIM ** -0.5) for hh in range(RET_HEADS)]
    vs = [_head(v, hh) for hh in range(RET_HEADS)]
    return qs, ks, vs


def _project(x, g_pre, w_in, cos_f, sin_f):
    h = _rms(x, g_pre).astype(_BF16)
    proj = _dot(h, w_in)
    q = proj[:, 0:RET_WIDTH]
    k = proj[:, RET_WIDTH:2 * RET_WIDTH]
    v = proj[:, 2 * RET_WIDTH:3 * RET_WIDTH]
    gate = proj[:, 3 * RET_WIDTH:4 * RET_WIDTH]
    ga = proj[:, 4 * RET_WIDTH:4 * RET_WIDTH + CONV_WIDTH]
    gb = proj[:, 4 * RET_WIDTH + CONV_WIDTH:N_IN]
    qs, ks, vs = _split_heads(q, k, v, cos_f, sin_f)
    return qs, ks, vs, gate, ga * jax.nn.sigmoid(gb)


def _group_norm_gate(ret_h, gate_h, g, b):
    mu = jnp.mean(ret_h, axis=-1, keepdims=True)
    d = ret_h - mu
    var = jnp.mean(d * d, axis=-1, keepdims=True)
    rn = d * lax.rsqrt(var + EPS) * g + b
    return rn * _silu(gate_h)


def _layer_norm(x, g, b):
    mu = jnp.mean(x, axis=-1, keepdims=True)
    d = x - mu
    var = jnp.mean(d * d, axis=-1, keepdims=True)
    return d * lax.rsqrt(var + EPS) * g + b


def _causal_conv_lanes(ext_ref, dww_ref, dwb_ref, y_ref, n_rows, c):
    rb = min(CONV_ROW_BLOCK, n_rows)
    lanes = slice(c * LANES, (c + 1) * LANES)
    taps = [dww_ref[k:k + 1, lanes] for k in range(CONV_K)]
    bias = dwb_ref[:, lanes]
    for r0 in range(0, n_rows, rb):
        acc = ext_ref[c, r0 + HIST_OFF:r0 + HIST_OFF + rb, :] * taps[0]
        for k in range(1, CONV_K):
            acc = acc + ext_ref[c, r0 + HIST_OFF + k:r0 + HIST_OFF + k + rb, :] * taps[k]
        y_ref[r0:r0 + rb, lanes] = acc + bias


def _causal_conv(ext_ref, dww_ref, dwb_ref, y_ref, n_rows):
    for c in range(CONV_LANE_BLOCKS):
        _causal_conv_lanes(ext_ref, dww_ref, dwb_ref, y_ref, n_rows, c)


def _ext_store(ext_ref, r0, rows_value):
    n = rows_value.shape[0]
    for c in range(CONV_LANE_BLOCKS):
        ext_ref[c, r0:r0 + n, :] = rows_value[:, c * LANES:(c + 1) * LANES]


def _ext_load(ext_ref, r0, n):
    return jnp.concatenate([ext_ref[c, r0:r0 + n, :] for c in range(CONV_LANE_BLOCKS)], axis=-1)


def _mix_out(x, ret_out, conv_out, wout_ref, g_post):
    half = wout_ref.shape[0] // 2
    mix = (_dot(conv_out.astype(_BF16), _weight(wout_ref[half:, :]))
           + _dot(ret_out.astype(_BF16), _weight(wout_ref[0:half, :])))
    return x + _rms(mix, g_post)


def _prompt_mixer_kernel(x_ref, tile_rot_ref, row_rot_ref, gpre_ref, win_ref, dmat_ref, din_ref, dout_ref,
                         sdec_ref, gng_ref, gnb_ref, dww_ref, dwb_ref, clng_ref, clnb_ref, wout_ref, gpost_ref,
                         s0_ref, c0_ref,
                         x1_ref, sfin_ref, cfin_ref,
                         s_scr, ext_scr, y_scr):
    i = pl.program_id(0)
    rows = x_ref.shape[0]

    @pl.when(i == 0)
    def _():
        s_scr[...] = s0_ref[...]
        _ext_store(ext_scr, 0, jnp.zeros((HIST_OFF, CONV_WIDTH), _F32))
        _ext_store(ext_scr, HIST_OFF, c0_ref[...])

    x = x_ref[...]
    h = _rms(x, gpre_ref[...]).astype(_BF16)

    glu = _dot(h, _weight(win_ref[:, 4 * RET_WIDTH:N_IN]))
    u = glu[:, 0:CONV_WIDTH] * jax.nn.sigmoid(glu[:, CONV_WIDTH:2 * CONV_WIDTH])
    _ext_store(ext_scr, HIST, u)
    blocks = []
    for c in range(CONV_LANE_BLOCKS):
        blocks.append(_dot(h, _weight(win_ref[:, c * RET_WIDTH:(c + 1) * RET_WIDTH])))
        _causal_conv_lanes(ext_scr, dww_ref, dwb_ref, y_scr, rows, c)
    q, k, v, gate = blocks
    ext_scr[:, 0:HIST, :] = ext_scr[:, rows:rows + HIST, :]
    conv_out = _silu(_layer_norm(y_scr[...], clng_ref[...], clnb_ref[...]))

    ret_rows = []
    for b in range(rows // RET_BLOCK):
        rs = slice(b * RET_BLOCK, (b + 1) * RET_BLOCK)
        cos_a, sin_a = tile_rot_ref[b, 0, 0:1, :], tile_rot_ref[b, 1, 0:1, :]
        cos_f = cos_a * row_rot_ref[0] - sin_a * row_rot_ref[1]
        sin_f = sin_a * row_rot_ref[2] + cos_a * row_rot_ref[3]
        qs, ks, vs = _split_heads(q[rs], k[rs], v[rs], cos_f, sin_f)
        ret_parts = []
        for hh in range(RET_HEADS):
            hl = slice(hh * HEAD_DIM, (hh + 1) * HEAD_DIM)
            qb = qs[hh].astype(_BF16)
            vb = vs[hh].astype(_BF16)
            p = (_dot_nt(qb, ks[hh].astype(_BF16)) * dmat_ref[hh]).astype(_BF16)
            intra = _dot(p, vb)
            s_old = s_scr[hh]
            inter = _dot(qb, s_old.astype(_BF16)) * din_ref[:, hl]
            kd = (ks[hh] * dout_ref[:, hl]).astype(_BF16)
            s_scr[hh] = s_old * sdec_ref[hh:hh + 1, :] + _dot_tn(kd, vb)
            ret_parts.append(_group_norm_gate(intra + inter, gate[rs, hl], gng_ref[:, hl], gnb_ref[:, hl]))
        ret_rows.append(jnp.concatenate(ret_parts, axis=-1))
    ret_out = jnp.concatenate(ret_rows, axis=0)

    x1_ref[...] = _mix_out(x, ret_out, conv_out, wout_ref, gpost_ref[...])

    @pl.when(i == pl.num_programs(0) - 1)
    def _():
        sfin_ref[...] = s_scr[...]
        cfin_ref[...] = _ext_load(ext_scr, HIST_OFF, CONV_K - 1)


def _const_spec(shape):
    zeros = (0,) * len(shape)
    return pl.BlockSpec(shape, lambda i, z=zeros: z)


def _resident_spec(shape):
    zeros = (0,) * len(shape)
    return pl.BlockSpec(shape, lambda i, z=zeros: z, pipeline_mode=pl.Buffered(1))


def _prompt_mixer(x, tile_rot, row_rot, g_pre, w_in, dmat, d_in, d_out, s_dec, gn_g, gn_b, dw_w, dw_b, cln_g, cln_b,
                  w_out, g_post, s0, c0):
    n = x.shape[0]
    t = PROMPT_TILE
    consts = (row_rot, g_pre, w_in, dmat, d_in, d_out, s_dec, gn_g, gn_b, dw_w, dw_b, cln_g, cln_b, w_out,
              g_post, s0, c0)
    big = (w_in, w_out, dmat)
    in_specs = ([pl.BlockSpec((t, D_MODEL), lambda i: (i, 0)),
                 pl.BlockSpec((t // RET_BLOCK,) + tile_rot.shape[1:], lambda i: (i, 0, 0, 0))]
                + [_resident_spec(c.shape) if any(c is b for b in big) else _const_spec(c.shape) for c in consts])
    out_shape = (jax.ShapeDtypeStruct((n, D_MODEL), _F32),
                 jax.ShapeDtypeStruct((RET_HEADS, HEAD_DIM, HEAD_DIM), _F32),
                 jax.ShapeDtypeStruct((CONV_K - 1, CONV_WIDTH), _F32))
    out_specs = (pl.BlockSpec((t, D_MODEL), lambda i: (i, 0)),
                 _const_spec((RET_HEADS, HEAD_DIM, HEAD_DIM)),
                 _const_spec((CONV_K - 1, CONV_WIDTH)))
    return pl.pallas_call(
        _prompt_mixer_kernel,
        grid=(n // t,),
        in_specs=in_specs,
        out_specs=out_specs,
        out_shape=out_shape,
        scratch_shapes=[pltpu.VMEM((RET_HEADS, HEAD_DIM, HEAD_DIM), _F32),
                        pltpu.VMEM((CONV_LANE_BLOCKS, HIST + t, LANES), _F32),
                        pltpu.VMEM((t, CONV_WIDTH), _F32)],
        compiler_params=pltpu.CompilerParams(dimension_semantics=("arbitrary",),
                                             vmem_limit_bytes=VMEM_LIMIT_BYTES),
        name="prompt_mixer",
    )(x, tile_rot, *consts)


def _sample_mixer_kernel(x_ref, cos_ref, sin_ref, gpre_ref, win_ref, dmat_ref, din_ref, dout_ref, sdec_ref,
                         gng_ref, gnb_ref, dww_ref, dwb_ref, clng_ref, clnb_ref, wout_ref, gpost_ref,
                         state_ref, cache_ref,
                         x1_ref, snew_ref, cnew_ref, smeta_ref, cmeta_ref,
                         ext_scr, y_scr):
    g = pl.program_id(0)
    has_state = g > 0

    x = x_ref[...]
    qs, ks, vs, gate, u = _project(x, gpre_ref[...], _weight(win_ref[...]), cos_ref[...], sin_ref[...])

    ret_parts = []
    for h in range(RET_HEADS):
        hl = slice(h * HEAD_DIM, (h + 1) * HEAD_DIM)
        qb = qs[h].astype(_BF16)
        vb = vs[h].astype(_BF16)
        p = (_dot_nt(qb, ks[h].astype(_BF16)) * dmat_ref[h]).astype(_BF16)
        intra = _dot(p, vb)
        kd = (ks[h] * dout_ref[:, hl]).astype(_BF16)
        inter_parts = []
        for s in range(GROUP_STREAMS):
            rs = slice(s * STREAM_ROWS, (s + 1) * STREAM_ROWS)
            s_old = jnp.where(has_state, state_ref[s, h], 0.0)
            inter_parts.append(_dot(qb[rs], s_old.astype(_BF16)))
            snew_ref[s, h] = s_old * sdec_ref[h:h + 1, :] + _dot_tn(kd[rs], vb[rs])
        inter = jnp.concatenate(inter_parts, axis=0) * din_ref[:, hl]
        ret_parts.append(_group_norm_gate(intra + inter, _head(gate, h), gng_ref[:, hl], gnb_ref[:, hl]))
    ret_out = jnp.concatenate(ret_parts, axis=-1)

    for s in range(GROUP_STREAMS):
        rs = slice(s * STREAM_ROWS, (s + 1) * STREAM_ROWS)
        _ext_store(ext_scr, 0, jnp.zeros((HIST_OFF, CONV_WIDTH), _F32))
        _ext_store(ext_scr, HIST_OFF, jnp.where(has_state, cache_ref[s], 0.0))
        _ext_store(ext_scr, HIST, u[rs])
        _causal_conv(ext_scr, dww_ref, dwb_ref, y_scr.at[rs], STREAM_ROWS)
        cnew_ref[s] = _ext_load(ext_scr, HIST + STREAM_ROWS - (CONV_K - 1), CONV_K - 1)
    conv_out = _silu(_layer_norm(y_scr[...], clng_ref[...], clnb_ref[...]))

    x1_ref[...] = _mix_out(x, ret_out, conv_out, wout_ref, gpost_ref[...])

    @pl.when(g == 0)
    def _():
        smeta_ref[...] = snew_ref[0]
        cmeta_ref[...] = cnew_ref[0]


def _sample_mixer(x_all, cos_f, sin_f, g_pre, w_in, dmat, d_in, d_out, s_dec, gn_g, gn_b, dw_w, dw_b, cln_g, cln_b,
                  w_out, g_post, state, cache):
    n = x_all.shape[0]
    n_streams = state.shape[0]
    consts = (g_pre, w_in, dmat, d_in, d_out, s_dec, gn_g, gn_b, dw_w, dw_b, cln_g, cln_b, w_out, g_post)
    stream_blk = lambda i: (jnp.maximum(i - 1, 0), 0, 0, 0)
    cache_blk = lambda i: (jnp.maximum(i - 1, 0), 0, 0)
    in_specs = ([pl.BlockSpec((GROUP_ROWS, D_MODEL), lambda i: (i, 0)),
                 pl.BlockSpec((GROUP_ROWS, HEAD_DIM), lambda i: (i, 0)),
                 pl.BlockSpec((GROUP_ROWS, HEAD_DIM), lambda i: (i, 0))]
                + [_const_spec(c.shape) for c in consts]
                + [pl.BlockSpec((GROUP_STREAMS, RET_HEADS, HEAD_DIM, HEAD_DIM), stream_blk),
                   pl.BlockSpec((GROUP_STREAMS, CONV_K - 1, CONV_WIDTH), cache_blk)])
    out_shape = (jax.ShapeDtypeStruct((n, D_MODEL), _F32),
                 jax.ShapeDtypeStruct((n_streams, RET_HEADS, HEAD_DIM, HEAD_DIM), _F32),
                 jax.ShapeDtypeStruct((n_streams, CONV_K - 1, CONV_WIDTH), _F32),
                 jax.ShapeDtypeStruct((RET_HEADS, HEAD_DIM, HEAD_DIM), _F32),
                 jax.ShapeDtypeStruct((CONV_K - 1, CONV_WIDTH), _F32))
    out_specs = (pl.BlockSpec((GROUP_ROWS, D_MODEL), lambda i: (i, 0)),
                 pl.BlockSpec((GROUP_STREAMS, RET_HEADS, HEAD_DIM, HEAD_DIM), stream_blk),
                 pl.BlockSpec((GROUP_STREAMS, CONV_K - 1, CONV_WIDTH), cache_blk),
                 _const_spec((RET_HEADS, HEAD_DIM, HEAD_DIM)),
                 _const_spec((CONV_K - 1, CONV_WIDTH)))
    return pl.pallas_call(
        _sample_mixer_kernel,
        grid=(n // GROUP_ROWS,),
        in_specs=in_specs,
        out_specs=out_specs,
        out_shape=out_shape,
        scratch_shapes=[pltpu.VMEM((CONV_LANE_BLOCKS, HIST + STREAM_ROWS, LANES), _F32),
                        pltpu.VMEM((GROUP_ROWS, CONV_WIDTH), _F32)],
        compiler_params=pltpu.CompilerParams(dimension_semantics=("arbitrary",),
                                             vmem_limit_bytes=VMEM_LIMIT_BYTES),
        name="sample_mixer",
    )(x_all, cos_f, sin_f, *consts, state, cache)


def _mlp_kernel(x_ref, gpre_ref, w1_ref, w2_ref, gpost_ref, y_ref):
    x = x_ref[...]
    h = _rms(x, gpre_ref[...]).astype(_BF16)
    a = jnp.maximum(_dot(h, _weight(w1_ref[...])), 0.0)
    f = _dot((a * a).astype(_BF16), _weight(w2_ref[...]))
    y_ref[...] = x + _rms(f, gpost_ref[...])


def _mlp(x, g_pre, w1, w2, g_post, tile):
    n = x.shape[0]
    return pl.pallas_call(
        _mlp_kernel,
        grid=(n // tile,),
        in_specs=[pl.BlockSpec((tile, D_MODEL), lambda i: (i, 0)),
                  _const_spec(g_pre.shape),
                  _resident_spec(w1.shape),
                  _resident_spec(w2.shape),
                  _const_spec(g_post.shape)],
        out_specs=pl.BlockSpec((tile, D_MODEL), lambda i: (i, 0)),
        out_shape=jax.ShapeDtypeStruct((n, D_MODEL), _F32),
        compiler_params=pltpu.CompilerParams(dimension_semantics=("arbitrary",),
                                             vmem_limit_bytes=VMEM_LIMIT_BYTES),
        name="mlp",
    )(x, g_pre, w1, w2, g_post)


def _pack_kernel(w_ref, o_ref):
    o_ref[...] = pltpu.bitcast(w_ref[...].astype(_BF16), jnp.uint32)


def _pack_weight(w):
    k, n = w.shape
    tk = k
    while tk * n * 4 > PACK_BLOCK_BYTES and tk % (4 * SUBLANES) == 0:
        tk //= 2
    return pl.pallas_call(
        _pack_kernel,
        grid=(k // tk,),
        in_specs=[pl.BlockSpec((tk, n), lambda i: (i, 0))],
        out_specs=pl.BlockSpec((tk // 2, n), lambda i: (i, 0)),
        out_shape=jax.ShapeDtypeStruct((k // 2, n), jnp.uint32),
        compiler_params=pltpu.CompilerParams(dimension_semantics=("arbitrary",),
                                             vmem_limit_bytes=VMEM_LIMIT_BYTES),
        name="pack_weight",
    )(w)


def _log_gamma():
    return jnp.log1p(-jnp.exp2(-5.0 - jnp.arange(RET_HEADS, dtype=_F32)))


def _rope_tables(pos):
    half = HEAD_DIM // 2
    inv = ROPE_BASE ** (-jnp.arange(half, dtype=_F32) / half)
    ang = pos.astype(_F32)[:, None] * inv[None, :]
    cos = jnp.cos(ang)
    sin = jnp.sin(ang)
    return jnp.concatenate([cos, cos], axis=-1), jnp.concatenate([-sin, sin], axis=-1)


def _prompt_rope_tables(n_tiles, tile):
    half = HEAD_DIM // 2
    inv = ROPE_BASE ** (-jnp.arange(half, dtype=_F32) / half)
    base = (N_META + tile * jnp.arange(n_tiles, dtype=jnp.int32)).astype(_F32)[:, None] * inv[None, :]
    off = jnp.arange(tile, dtype=jnp.int32).astype(_F32)[:, None] * inv[None, :]
    dup = lambda a: jnp.concatenate([a, a], axis=-1)
    sign = jnp.concatenate([-jnp.ones((half,), _F32), jnp.ones((half,), _F32)])
    tile_rot = jnp.stack([dup(jnp.cos(base)), dup(jnp.sin(base))], axis=1)
    tile_rot = jnp.broadcast_to(tile_rot[:, :, None, :], (n_tiles, 2, SUBLANES, HEAD_DIM))
    cos_o, sin_o = dup(jnp.cos(off)), dup(jnp.sin(off))
    row_rot = jnp.stack([cos_o, sin_o, sign * cos_o, sign * sin_o])
    return tile_rot, row_rot


def _lanes(per_head):
    return jnp.repeat(per_head, HEAD_DIM, axis=1)


def _decay_tables(rows, segment, causal_chunk):
    lg = _log_gamma()
    idx = jnp.arange(rows)
    loc = (idx % segment).astype(_F32)
    same_seg = (idx[:, None] // segment) == (idx[None, :] // segment)
    visible = same_seg & ((idx[None, :] // causal_chunk) <= (idx[:, None] // causal_chunk))
    dist = jnp.abs(idx[:, None] - idx[None, :]).astype(_F32)
    dmat = jnp.where(visible[None], jnp.exp(lg[:, None, None] * dist[None]), 0.0)
    d_in = _lanes(jnp.exp(lg[None, :] * (loc[:, None] + 1.0)))
    d_out = _lanes(jnp.exp(lg[None, :] * (segment - 1.0 - loc[:, None])))
    s_dec = jnp.broadcast_to(jnp.exp(lg * segment)[:, None], (RET_HEADS, HEAD_DIM))
    return dmat, d_in, d_out, s_dec


def kernel(x_prompt, x_sample, state_ret, cache_conv, meta, g_pre_mix, w_in, gn_g, gn_b, dw_w, dw_b, cln_g, cln_b,
           w_out, g_post_mix, g_pre_mlp, w_mlp_in, w_mlp_out, g_post_mlp):
    batch, seq, _ = x_prompt.shape
    dec_batch, dec_seq, _ = x_sample.shape
    assert batch == 1 and dec_seq == STREAM_ROWS and meta.shape[0] == N_META == STREAM_ROWS
    assert seq % PROMPT_TILE == 0 and seq % MLP_TILE == 0 and dec_batch % GROUP_STREAMS == 0
    assert state_ret.shape[0] == 1

    l = 0
    mixer_w = (g_pre_mix[l][None], _pack_weight(w_in[l]))
    post_w = (gn_g[l][None], gn_b[l][None], dw_w[l], dw_b[l][None], cln_g[l][None], cln_b[l][None],
              _pack_weight(w_out[l]), g_post_mix[l][None])
    mlp_w = (g_pre_mlp[l][None], _pack_weight(w_mlp_in[l]), _pack_weight(w_mlp_out[l]), g_post_mlp[l][None])

    pad_rows = GROUP_ROWS - N_META
    x_all = jnp.concatenate([meta, jnp.zeros((pad_rows, D_MODEL), _F32),
                             x_sample.reshape(dec_batch * dec_seq, D_MODEL)], axis=0)
    pos_group0 = jnp.arange(GROUP_ROWS, dtype=jnp.int32)
    pos_stream = N_META + PAST_LEN + jnp.arange(dec_seq, dtype=jnp.int32)
    pos_all = jnp.concatenate([pos_group0, jnp.tile(pos_stream, dec_batch)])
    cos_s, sin_s = _rope_tables(pos_all)
    dec_s = _decay_tables(GROUP_ROWS, STREAM_ROWS, STREAM_ROWS)
    xs1, s_s, c_s, s_meta, c_meta = _sample_mixer(x_all, cos_s, sin_s, *mixer_w, *dec_s, *post_w,
                                                  state_ret[l], cache_conv[l])
    ys = _mlp(xs1, *mlp_w, tile=xs1.shape[0])[GROUP_ROWS:]

    rot_p = _prompt_rope_tables(seq // RET_BLOCK, RET_BLOCK)
    dec_p = _decay_tables(RET_BLOCK, RET_BLOCK, CHUNK)
    xp1, s_p, c_p = _prompt_mixer(x_prompt[0], *rot_p, *mixer_w, *dec_p, *post_w, s_meta, c_meta)
    yp = _mlp(xp1, *mlp_w, tile=MLP_TILE)

    return (yp[None], ys.reshape(dec_batch, dec_seq, D_MODEL), s_p[None, None], c_p[None, None],
            s_s[None], c_s[None])
```

```python
import functools

import jax
import jax.numpy as jnp
from jax import lax
from jax.experimental import pallas as pl
from jax.experimental.pallas import tpu as pltpu

D_MODEL = 1024
CHUNK = 64
N_META = 16
PAST_LEN = 4096
RET_WIDTH = 512
RET_HEADS = 4
HEAD_DIM = 128
CONV_WIDTH = 512
CONV_K = 31
D_FF = 4096
N_IN = 4 * RET_WIDTH + 2 * CONV_WIDTH
EPS = 1e-6
ROPE_BASE = 10000.0

LANES = 128
SUBLANES = 8
HIST = 32
HIST_OFF = HIST - (CONV_K - 1)
RET_BLOCK = 256
PROMPT_TILE = 512
STREAM_ROWS = 16
SAMPLE_GROUP_STREAMS = 16
MLP_TILE = 512
PACK_BLOCK_BYTES = 4 * 1024 * 1024
CONV_ROW_BLOCK = 32
CONV_LANE_BLOCKS = CONV_WIDTH // LANES
VMEM_LIMIT_BYTES = 56 * 1024 * 1024

_F32 = jnp.float32
_BF16 = jnp.bfloat16


def _rms(x, g):
    return x * lax.rsqrt(jnp.mean(x * x, axis=-1, keepdims=True) + EPS) * g


def _silu(x):
    return x * jax.nn.sigmoid(x)


def _dot(a, b):
    return jnp.dot(a, b, preferred_element_type=_F32)


def _dot_nt(a, b):
    return lax.dot_general(a, b, (((1,), (1,)), ((), ())), preferred_element_type=_F32)


def _dot_tn(a, b):
    return lax.dot_general(a, b, (((0,), (0,)), ((), ())), preferred_element_type=_F32)


def _weight(packed):
    return pltpu.bitcast(packed, _BF16)


def _head(x, h):
    return x[:, h * HEAD_DIM:(h + 1) * HEAD_DIM]


def _rope(xh, cos_f, sin_f):
    return xh * cos_f + pltpu.roll(xh, HEAD_DIM // 2, 1) * sin_f


def _split_heads(q, k, v, cos_f, sin_f):
    qs = [_rope(_head(q, hh), cos_f, sin_f) for hh in range(RET_HEADS)]
    ks = [_rope(_head(k, hh), cos_f, sin_f) * (HEAD_DIM ** -0.5) for hh in range(RET_HEADS)]
    vs = [_head(v, hh) for hh in range(RET_HEADS)]
    return qs, ks, vs


def _project(x, g_pre, w_in, cos_f, sin_f):
    h = _rms(x, g_pre).astype(_BF16)
    proj = _dot(h, w_in)
    q = proj[:, 0:RET_WIDTH]
    k = proj[:, RET_WIDTH:2 * RET_WIDTH]
    v = proj[:, 2 * RET_WIDTH:3 * RET_WIDTH]
    gate = proj[:, 3 * RET_WIDTH:4 * RET_WIDTH]
    ga = proj[:, 4 * RET_WIDTH:4 * RET_WIDTH + CONV_WIDTH]
    gb = proj[:, 4 * RET_WIDTH + CONV_WIDTH:N_IN]
    qs, ks, vs = _split_heads(q, k, v, cos_f, sin_f)
    return qs, ks, vs, gate, ga * jax.nn.sigmoid(gb)


def _group_norm_gate(ret_h, gate_h, g, b):
    mu = jnp.mean(ret_h, axis=-1, keepdims=True)
    d = ret_h - mu
    var = jnp.mean(d * d, axis=-1, keepdims=True)
    rn = d * lax.rsqrt(var + EPS) * g + b
    return rn * _silu(gate_h)


def _layer_norm(x, g, b):
    mu = jnp.mean(x, axis=-1, keepdims=True)
    d = x - mu
    var = jnp.mean(d * d, axis=-1, keepdims=True)
    return d * lax.rsqrt(var + EPS) * g + b


def _causal_conv_lanes(ext_ref, dww_ref, dwb_ref, y_ref, n_rows, c):
    rb = min(CONV_ROW_BLOCK, n_rows)
    lanes = slice(c * LANES, (c + 1) * LANES)
    taps = [dww_ref[k:k + 1, lanes] for k in range(CONV_K)]
    bias = dwb_ref[:, lanes]
    for r0 in range(0, n_rows, rb):
        acc = ext_ref[c, r0 + HIST_OFF:r0 + HIST_OFF + rb, :] * taps[0]
        for k in range(1, CONV_K):
            acc = acc + ext_ref[c, r0 + HIST_OFF + k:r0 + HIST_OFF + k + rb, :] * taps[k]
        y_ref[r0:r0 + rb, lanes] = acc + bias


def _causal_conv(ext_ref, dww_ref, dwb_ref, y_ref, n_rows):
    for c in range(CONV_LANE_BLOCKS):
        _causal_conv_lanes(ext_ref, dww_ref, dwb_ref, y_ref, n_rows, c)


def _ext_store(ext_ref, r0, rows_value):
    n = rows_value.shape[0]
    for c in range(CONV_LANE_BLOCKS):
        ext_ref[c, r0:r0 + n, :] = rows_value[:, c * LANES:(c + 1) * LANES]


def _ext_load(ext_ref, r0, n):
    return jnp.concatenate([ext_ref[c, r0:r0 + n, :] for c in range(CONV_LANE_BLOCKS)], axis=-1)


def _mix_out(x, ret_out, conv_out, wout_ref, g_post):
    half = wout_ref.shape[0] // 2
    mix = (_dot(conv_out.astype(_BF16), _weight(wout_ref[half:, :]))
           + _dot(ret_out.astype(_BF16), _weight(wout_ref[0:half, :])))
    return x + _rms(mix, g_post)


def _prompt_mixer_kernel(x_ref, tile_rot_ref, row_rot_ref, gpre_ref, win_ref, dmat_ref, din_ref, dout_ref,
                         sdec_ref, gng_ref, gnb_ref, dww_ref, dwb_ref, clng_ref, clnb_ref, wout_ref, gpost_ref,
                         s0_ref, c0_ref,
                         x1_ref, sfin_ref, cfin_ref,
                         s_scr, ext_scr, y_scr):
    i = pl.program_id(0)
    rows = x_ref.shape[0]

    @pl.when(i == 0)
    def _():
        s_scr[...] = s0_ref[...]
        _ext_store(ext_scr, 0, jnp.zeros((HIST_OFF, CONV_WIDTH), _F32))
        _ext_store(ext_scr, HIST_OFF, c0_ref[...])

    x = x_ref[...]
    h = _rms(x, gpre_ref[...]).astype(_BF16)

    glu = _dot(h, _weight(win_ref[:, 4 * RET_WIDTH:N_IN]))
    u = glu[:, 0:CONV_WIDTH] * jax.nn.sigmoid(glu[:, CONV_WIDTH:2 * CONV_WIDTH])
    _ext_store(ext_scr, HIST, u)
    blocks = []
    for c in range(CONV_LANE_BLOCKS):
        blocks.append(_dot(h, _weight(win_ref[:, c * RET_WIDTH:(c + 1) * RET_WIDTH])))
        _causal_conv_lanes(ext_scr, dww_ref, dwb_ref, y_scr, rows, c)
    q, k, v, gate = blocks
    ext_scr[:, 0:HIST, :] = ext_scr[:, rows:rows + HIST, :]
    conv_out = _silu(_layer_norm(y_scr[...], clng_ref[...], clnb_ref[...]))

    ret_rows = []
    for b in range(rows // RET_BLOCK):
        rs = slice(b * RET_BLOCK, (b + 1) * RET_BLOCK)
        cos_a, sin_a = tile_rot_ref[b, 0, 0:1, :], tile_rot_ref[b, 1, 0:1, :]
        cos_f = cos_a * row_rot_ref[0] - sin_a * row_rot_ref[1]
        sin_f = sin_a * row_rot_ref[2] + cos_a * row_rot_ref[3]
        qs, ks, vs = _split_heads(q[rs], k[rs], v[rs], cos_f, sin_f)
        ret_parts = []
        for hh in range(RET_HEADS):
            hl = slice(hh * HEAD_DIM, (hh + 1) * HEAD_DIM)
            qb = qs[hh].astype(_BF16)
            vb = vs[hh].astype(_BF16)
            p = (_dot_nt(qb, ks[hh].astype(_BF16)) * dmat_ref[hh]).astype(_BF16)
            intra = _dot(p, vb)
            s_old = s_scr[hh]
            inter = _dot(qb, s_old.astype(_BF16)) * din_ref[:, hl]
            kd = (ks[hh] * dout_ref[:, hl]).astype(_BF16)
            s_scr[hh] = s_old * sdec_ref[hh:hh + 1, :] + _dot_tn(kd, vb)
            ret_parts.append(_group_norm_gate(intra + inter, gate[rs, hl], gng_ref[:, hl], gnb_ref[:, hl]))
        ret_rows.append(jnp.concatenate(ret_parts, axis=-1))
    ret_out = jnp.concatenate(ret_rows, axis=0)

    x1_ref[...] = _mix_out(x, ret_out, conv_out, wout_ref, gpost_ref[...])

    @pl.when(i == pl.num_programs(0) - 1)
    def _():
        sfin_ref[...] = s_scr[...]
        cfin_ref[...] = _ext_load(ext_scr, HIST_OFF, CONV_K - 1)


def _const_spec(shape):
    zeros = (0,) * len(shape)
    return pl.BlockSpec(shape, lambda i, z=zeros: z)


def _resident_spec(shape):
    zeros = (0,) * len(shape)
    return pl.BlockSpec(shape, lambda i, z=zeros: z, pipeline_mode=pl.Buffered(1))


def _prompt_mixer(x, tile_rot, row_rot, g_pre, w_in, dmat, d_in, d_out, s_dec, gn_g, gn_b, dw_w, dw_b, cln_g, cln_b,
                  w_out, g_post, s0, c0):
    n = x.shape[0]
    t = PROMPT_TILE
    consts = (row_rot, g_pre, w_in, dmat, d_in, d_out, s_dec, gn_g, gn_b, dw_w, dw_b, cln_g, cln_b, w_out,
              g_post, s0, c0)
    big = (w_in, w_out, dmat)
    in_specs = ([pl.BlockSpec((t, D_MODEL), lambda i: (i, 0)),
                 pl.BlockSpec((t // RET_BLOCK,) + tile_rot.shape[1:], lambda i: (i, 0, 0, 0))]
                + [_resident_spec(c.shape) if any(c is b for b in big) else _const_spec(c.shape) for c in consts])
    out_shape = (jax.ShapeDtypeStruct((n, D_MODEL), _F32),
                 jax.ShapeDtypeStruct((RET_HEADS, HEAD_DIM, HEAD_DIM), _F32),
                 jax.ShapeDtypeStruct((CONV_K - 1, CONV_WIDTH), _F32))
    out_specs = (pl.BlockSpec((t, D_MODEL), lambda i: (i, 0)),
                 _const_spec((RET_HEADS, HEAD_DIM, HEAD_DIM)),
                 _const_spec((CONV_K - 1, CONV_WIDTH)))
    return pl.pallas_call(
        _prompt_mixer_kernel,
        grid=(n // t,),
        in_specs=in_specs,
        out_specs=out_specs,
        out_shape=out_shape,
        scratch_shapes=[pltpu.VMEM((RET_HEADS, HEAD_DIM, HEAD_DIM), _F32),
                        pltpu.VMEM((CONV_LANE_BLOCKS, HIST + t, LANES), _F32),
                        pltpu.VMEM((t, CONV_WIDTH), _F32)],
        compiler_params=pltpu.CompilerParams(dimension_semantics=("arbitrary",),
                                             vmem_limit_bytes=VMEM_LIMIT_BYTES),
        name="prompt_mixer",
    )(x, tile_rot, *consts)


def _sample_mixer_kernel(x_ref, meta_ref, cos_ref, sin_ref, gpre_ref, win_ref, dmat_ref, din_ref, dout_ref, sdec_ref,
                         gng_ref, gnb_ref, dww_ref, dwb_ref, clng_ref, clnb_ref, wout_ref, gpost_ref,
                         state_ref, cache_ref,
                         x1_ref, snew_ref, cnew_ref, smeta_ref, cmeta_ref,
                         ext_scr, y_scr):
    n_streams = state_ref.shape[0]
    stream_rows = x_ref.shape[0]

    x = jnp.concatenate([x_ref[...], meta_ref[...]], axis=0)
    qs, ks, vs, gate, u = _project(x, gpre_ref[...], _weight(win_ref[...]), cos_ref[...], sin_ref[...])

    ret_parts = []
    for h in range(RET_HEADS):
        hl = slice(h * HEAD_DIM, (h + 1) * HEAD_DIM)
        qb = qs[h].astype(_BF16)
        vb = vs[h].astype(_BF16)
        p = (_dot_nt(qb, ks[h].astype(_BF16)) * dmat_ref[h]).astype(_BF16)
        intra = _dot(p, vb)
        kd = (ks[h] * dout_ref[:, hl]).astype(_BF16)
        inter_parts = []
        for s in range(n_streams):
            rs = slice(s * STREAM_ROWS, (s + 1) * STREAM_ROWS)
            s_old = state_ref[s, h]
            inter_parts.append(_dot(qb[rs], s_old.astype(_BF16)))
            snew_ref[s, h] = s_old * sdec_ref[h:h + 1, :] + _dot_tn(kd[rs], vb[rs])
        ms = slice(stream_rows, stream_rows + STREAM_ROWS)
        inter_parts.append(jnp.zeros((STREAM_ROWS, HEAD_DIM), _F32))
        smeta_ref[h] = _dot_tn(kd[ms], vb[ms])
        inter = jnp.concatenate(inter_parts, axis=0) * din_ref[:, hl]
        ret_parts.append(_group_norm_gate(intra + inter, _head(gate, h), gng_ref[:, hl], gnb_ref[:, hl]))
    ret_out = jnp.concatenate(ret_parts, axis=-1)

    tail = slice(HIST + STREAM_ROWS - (CONV_K - 1), HIST + STREAM_ROWS)
    for s in range(n_streams + 1):
        rs = slice(s * STREAM_ROWS, (s + 1) * STREAM_ROWS)
        _ext_store(ext_scr, 0, jnp.zeros((HIST_OFF, CONV_WIDTH), _F32))
        if s < n_streams:
            _ext_store(ext_scr, HIST_OFF, cache_ref[s])
        else:
            _ext_store(ext_scr, HIST_OFF, jnp.zeros((CONV_K - 1, CONV_WIDTH), _F32))
        _ext_store(ext_scr, HIST, u[rs])
        _causal_conv(ext_scr, dww_ref, dwb_ref, y_scr.at[rs], STREAM_ROWS)
        new_cache = _ext_load(ext_scr, tail.start, CONV_K - 1)
        if s < n_streams:
            cnew_ref[s] = new_cache
        else:
            cmeta_ref[...] = new_cache
    conv_out = _silu(_layer_norm(y_scr[...], clng_ref[...], clnb_ref[...]))

    x1_ref[...] = _mix_out(x, ret_out, conv_out, wout_ref, gpost_ref[...])[0:stream_rows]


def _sample_mixer(x_rows, meta, cos_f, sin_f, g_pre, w_in, dmat, d_in, d_out, s_dec, gn_g, gn_b, dw_w, dw_b, cln_g,
                  cln_b, w_out, g_post, state, cache):
    n = x_rows.shape[0]
    n_streams = state.shape[0]
    group = SAMPLE_GROUP_STREAMS
    rows = group * STREAM_ROWS
    consts = (meta, cos_f, sin_f, g_pre, w_in, dmat, d_in, d_out, s_dec, gn_g, gn_b, dw_w, dw_b, cln_g, cln_b, w_out,
              g_post)
    big = (w_in, w_out)
    in_specs = ([pl.BlockSpec((rows, D_MODEL), lambda i: (i, 0))]
                + [_resident_spec(c.shape) if any(c is b for b in big) else _const_spec(c.shape) for c in consts]
                + [pl.BlockSpec((group, RET_HEADS, HEAD_DIM, HEAD_DIM), lambda i: (i, 0, 0, 0)),
                   pl.BlockSpec((group, CONV_K - 1, CONV_WIDTH), lambda i: (i, 0, 0))])
    out_shape = (jax.ShapeDtypeStruct((n, D_MODEL), _F32),
                 jax.ShapeDtypeStruct((n_streams, RET_HEADS, HEAD_DIM, HEAD_DIM), _F32),
                 jax.ShapeDtypeStruct((n_streams, CONV_K - 1, CONV_WIDTH), _F32),
                 jax.ShapeDtypeStruct((RET_HEADS, HEAD_DIM, HEAD_DIM), _F32),
                 jax.ShapeDtypeStruct((CONV_K - 1, CONV_WIDTH), _F32))
    out_specs = (pl.BlockSpec((rows, D_MODEL), lambda i: (i, 0)),
                 pl.BlockSpec((group, RET_HEADS, HEAD_DIM, HEAD_DIM), lambda i: (i, 0, 0, 0)),
                 pl.BlockSpec((group, CONV_K - 1, CONV_WIDTH), lambda i: (i, 0, 0)),
                 _const_spec((RET_HEADS, HEAD_DIM, HEAD_DIM)),
                 _const_spec((CONV_K - 1, CONV_WIDTH)))
    return pl.pallas_call(
        _sample_mixer_kernel,
        grid=(n_streams // group,),
        in_specs=in_specs,
        out_specs=out_specs,
        out_shape=out_shape,
        scratch_shapes=[pltpu.VMEM((CONV_LANE_BLOCKS, HIST + STREAM_ROWS, LANES), _F32),
                        pltpu.VMEM((rows + STREAM_ROWS, CONV_WIDTH), _F32)],
        compiler_params=pltpu.CompilerParams(dimension_semantics=("arbitrary",),
                                             vmem_limit_bytes=VMEM_LIMIT_BYTES),
        name="sample_mixer",
    )(x_rows, *consts, state, cache)


def _mlp_rows(x, g_pre, w1_ref, w2_ref, g_post):
    h = _rms(x, g_pre).astype(_BF16)
    a = jnp.maximum(_dot(h, _weight(w1_ref[...])), 0.0)
    f = _dot((a * a).astype(_BF16), _weight(w2_ref[...]))
    return x + _rms(f, g_post)


def _mlp_kernel(xa_ref, xb_ref, gpre_ref, w1_ref, w2_ref, gpost_ref, ya_ref, yb_ref, *, a_tiles):
    i = pl.program_id(0)

    @pl.when(i < a_tiles)
    def _():
        ya_ref[...] = _mlp_rows(xa_ref[...], gpre_ref[...], w1_ref, w2_ref, gpost_ref[...])

    @pl.when(i >= a_tiles)
    def _():
        yb_ref[...] = _mlp_rows(xb_ref[...], gpre_ref[...], w1_ref, w2_ref, gpost_ref[...])


def _mlp(xa, xb, g_pre, w1, w2, g_post):
    t = MLP_TILE
    a_tiles, b_tiles = xa.shape[0] // t, xb.shape[0] // t
    a_blk = lambda i: (jnp.minimum(i, a_tiles - 1), 0)
    b_blk = lambda i: (jnp.maximum(i - a_tiles, 0), 0)
    return pl.pallas_call(
        functools.partial(_mlp_kernel, a_tiles=a_tiles),
        grid=(a_tiles + b_tiles,),
        in_specs=[pl.BlockSpec((t, D_MODEL), a_blk),
                  pl.BlockSpec((t, D_MODEL), b_blk),
                  _const_spec(g_pre.shape),
                  _resident_spec(w1.shape),
                  _resident_spec(w2.shape),
                  _const_spec(g_post.shape)],
        out_specs=(pl.BlockSpec((t, D_MODEL), a_blk), pl.BlockSpec((t, D_MODEL), b_blk)),
        out_shape=(jax.ShapeDtypeStruct(xa.shape, _F32), jax.ShapeDtypeStruct(xb.shape, _F32)),
        compiler_params=pltpu.CompilerParams(dimension_semantics=("arbitrary",),
                                             vmem_limit_bytes=VMEM_LIMIT_BYTES),
        name="mlp",
    )(xa, xb, g_pre, w1, w2, g_post)


def _pack_kernel(w_ref, o_ref):
    o_ref[...] = pltpu.bitcast(w_ref[...].astype(_BF16), jnp.uint32)


def _pack_weight(w):
    k, n = w.shape
    tk = k
    while tk * n * 4 > PACK_BLOCK_BYTES and tk % (4 * SUBLANES) == 0:
        tk //= 2
    return pl.pallas_call(
        _pack_kernel,
        grid=(k // tk,),
        in_specs=[pl.BlockSpec((tk, n), lambda i: (i, 0))],
        out_specs=pl.BlockSpec((tk // 2, n), lambda i: (i, 0)),
        out_shape=jax.ShapeDtypeStruct((k // 2, n), jnp.uint32),
        compiler_params=pltpu.CompilerParams(dimension_semantics=("arbitrary",),
                                             vmem_limit_bytes=VMEM_LIMIT_BYTES),
        name="pack_weight",
    )(w)


def _log_gamma():
    return jnp.log1p(-jnp.exp2(-5.0 - jnp.arange(RET_HEADS, dtype=_F32)))


def _rope_tables(pos):
    half = HEAD_DIM // 2
    inv = ROPE_BASE ** (-jnp.arange(half, dtype=_F32) / half)
    ang = pos.astype(_F32)[:, None] * inv[None, :]
    cos = jnp.cos(ang)
    sin = jnp.sin(ang)
    return jnp.concatenate([cos, cos], axis=-1), jnp.concatenate([-sin, sin], axis=-1)


def _prompt_rope_tables(n_tiles, tile):
    half = HEAD_DIM // 2
    inv = ROPE_BASE ** (-jnp.arange(half, dtype=_F32) / half)
    base = (N_META + tile * jnp.arange(n_tiles, dtype=jnp.int32)).astype(_F32)[:, None] * inv[None, :]
    off = jnp.arange(tile, dtype=jnp.int32).astype(_F32)[:, None] * inv[None, :]
    dup = lambda a: jnp.concatenate([a, a], axis=-1)
    sign = jnp.concatenate([-jnp.ones((half,), _F32), jnp.ones((half,), _F32)])
    tile_rot = jnp.stack([dup(jnp.cos(base)), dup(jnp.sin(base))], axis=1)
    tile_rot = jnp.broadcast_to(tile_rot[:, :, None, :], (n_tiles, 2, SUBLANES, HEAD_DIM))
    cos_o, sin_o = dup(jnp.cos(off)), dup(jnp.sin(off))
    row_rot = jnp.stack([cos_o, sin_o, sign * cos_o, sign * sin_o])
    return tile_rot, row_rot


def _lanes(per_head):
    return jnp.repeat(per_head, HEAD_DIM, axis=1)


def _decay_tables(rows, segment, causal_chunk):
    lg = _log_gamma()
    idx = jnp.arange(rows)
    loc = (idx % segment).astype(_F32)
    same_seg = (idx[:, None] // segment) == (idx[None, :] // segment)
    visible = same_seg & ((idx[None, :] // causal_chunk) <= (idx[:, None] // causal_chunk))
    dist = jnp.abs(idx[:, None] - idx[None, :]).astype(_F32)
    dmat = jnp.where(visible[None], jnp.exp(lg[:, None, None] * dist[None]), 0.0)
    d_in = _lanes(jnp.exp(lg[None, :] * (loc[:, None] + 1.0)))
    d_out = _lanes(jnp.exp(lg[None, :] * (segment - 1.0 - loc[:, None])))
    s_dec = jnp.broadcast_to(jnp.exp(lg * segment)[:, None], (RET_HEADS, HEAD_DIM))
    return dmat, d_in, d_out, s_dec


def kernel(x_prompt, x_sample, state_ret, cache_conv, meta, g_pre_mix, w_in, gn_g, gn_b, dw_w, dw_b, cln_g, cln_b,
           w_out, g_post_mix, g_pre_mlp, w_mlp_in, w_mlp_out, g_post_mlp):
    batch, seq, _ = x_prompt.shape
    dec_batch, dec_seq, _ = x_sample.shape
    assert batch == 1 and dec_seq == STREAM_ROWS and meta.shape[0] == N_META == STREAM_ROWS
    assert seq % PROMPT_TILE == 0 and seq % MLP_TILE == 0 and dec_batch % SAMPLE_GROUP_STREAMS == 0
    assert (dec_batch * dec_seq) % MLP_TILE == 0 and state_ret.shape[0] == 1

    l = 0
    mixer_w = (g_pre_mix[l][None], _pack_weight(w_in[l]))
    post_w = (gn_g[l][None], gn_b[l][None], dw_w[l], dw_b[l][None], cln_g[l][None], cln_b[l][None],
              _pack_weight(w_out[l]), g_post_mix[l][None])
    mlp_w = (g_pre_mlp[l][None], _pack_weight(w_mlp_in[l]), _pack_weight(w_mlp_out[l]), g_post_mlp[l][None])

    group_rows = (SAMPLE_GROUP_STREAMS + 1) * STREAM_ROWS
    pos_stream = N_META + PAST_LEN + jnp.arange(dec_seq, dtype=jnp.int32)
    pos_group = jnp.concatenate([jnp.tile(pos_stream, SAMPLE_GROUP_STREAMS), jnp.arange(N_META, dtype=jnp.int32)])
    cos_s, sin_s = _rope_tables(pos_group)
    dec_s = _decay_tables(group_rows, STREAM_ROWS, STREAM_ROWS)
    xs1, s_s, c_s, s_meta, c_meta = _sample_mixer(x_sample.reshape(dec_batch * dec_seq, D_MODEL), meta, cos_s, sin_s,
                                                  *mixer_w, *dec_s, *post_w, state_ret[l], cache_conv[l])

    rot_p = _prompt_rope_tables(seq // RET_BLOCK, RET_BLOCK)
    dec_p = _decay_tables(RET_BLOCK, RET_BLOCK, CHUNK)
    xp1, s_p, c_p = _prompt_mixer(x_prompt[0], *rot_p, *mixer_w, *dec_p, *post_w, s_meta, c_meta)
    yp, ys = _mlp(xp1, xs1, *mlp_w)

    return (yp[None], ys.reshape(dec_batch, dec_seq, D_MODEL), s_p[None, None], c_p[None, None],
            s_s[None], c_s[None])
```

```python
import functools

import jax
import jax.numpy as jnp
from jax import lax
from jax.experimental import pallas as pl
from jax.experimental.pallas import tpu as pltpu

D_MODEL = 1024
CHUNK = 64
N_META = 16
PAST_LEN = 4096
RET_WIDTH = 512
RET_HEADS = 4
HEAD_DIM = 128
CONV_WIDTH = 512
CONV_K = 31
D_FF = 4096
N_IN = 4 * RET_WIDTH + 2 * CONV_WIDTH
EPS = 1e-6
ROPE_BASE = 10000.0

LANES = 128
SUBLANES = 8
HIST = 32
HIST_OFF = HIST - (CONV_K - 1)
RET_BLOCK = 256
PROMPT_TILE = 1024
STREAM_ROWS = 16
SAMPLE_GROUP_STREAMS = 8
MLP_TILE = 512
PACK_BLOCK_BYTES = 4 * 1024 * 1024
CONV_ROW_BLOCK = 32
CONV_LANE_BLOCKS = CONV_WIDTH // LANES
VMEM_LIMIT_BYTES = 56 * 1024 * 1024

_F32 = jnp.float32
_BF16 = jnp.bfloat16


def _rms(x, g):
    return x * lax.rsqrt(jnp.mean(x * x, axis=-1, keepdims=True) + EPS) * g


def _silu(x):
    return x * jax.nn.sigmoid(x)


def _dot(a, b):
    return jnp.dot(a, b, preferred_element_type=_F32)


def _dot_nt(a, b):
    return lax.dot_general(a, b, (((1,), (1,)), ((), ())), preferred_element_type=_F32)


def _dot_tn(a, b):
    return lax.dot_general(a, b, (((0,), (0,)), ((), ())), preferred_element_type=_F32)


def _weight(packed):
    return pltpu.bitcast(packed, _BF16)


def _head(x, h):
    return x[:, h * HEAD_DIM:(h + 1) * HEAD_DIM]


def _rope(xh, cos_f, sin_f):
    return xh * cos_f + pltpu.roll(xh, HEAD_DIM // 2, 1) * sin_f


def _split_heads(q, k, v, cos_f, sin_f):
    qs = [_rope(_head(q, hh), cos_f, sin_f) for hh in range(RET_HEADS)]
    ks = [_rope(_head(k, hh), cos_f, sin_f) * (HEAD_DIM ** -0.5) for hh in range(RET_HEADS)]
    vs = [_head(v, hh) for hh in range(RET_HEADS)]
    return qs, ks, vs


def _project(x, g_pre, w_in, cos_f, sin_f):
    h = _rms(x, g_pre).astype(_BF16)
    proj = _dot(h, w_in)
    q = proj[:, 0:RET_WIDTH]
    k = proj[:, RET_WIDTH:2 * RET_WIDTH]
    v = proj[:, 2 * RET_WIDTH:3 * RET_WIDTH]
    gate = proj[:, 3 * RET_WIDTH:4 * RET_WIDTH]
    ga = proj[:, 4 * RET_WIDTH:4 * RET_WIDTH + CONV_WIDTH]
    gb = proj[:, 4 * RET_WIDTH + CONV_WIDTH:N_IN]
    qs, ks, vs = _split_heads(q, k, v, cos_f, sin_f)
    return qs, ks, vs, gate, ga * jax.nn.sigmoid(gb)


def _group_norm_gate(ret_h, gate_h, g, b):
    mu = jnp.mean(ret_h, axis=-1, keepdims=True)
    d = ret_h - mu
    var = jnp.mean(d * d, axis=-1, keepdims=True)
    rn = d * lax.rsqrt(var + EPS) * g + b
    return rn * _silu(gate_h)


def _layer_norm(x, g, b):
    mu = jnp.mean(x, axis=-1, keepdims=True)
    d = x - mu
    var = jnp.mean(d * d, axis=-1, keepdims=True)
    return d * lax.rsqrt(var + EPS) * g + b


def _causal_conv_lanes(ext_ref, dww_ref, dwb_ref, y_ref, n_rows, c):
    rb = min(CONV_ROW_BLOCK, n_rows)
    lanes = slice(c * LANES, (c + 1) * LANES)
    taps = [dww_ref[k:k + 1, lanes] for k in range(CONV_K)]
    bias = dwb_ref[:, lanes]
    for r0 in range(0, n_rows, rb):
        acc = ext_ref[c, r0 + HIST_OFF:r0 + HIST_OFF + rb, :] * taps[0]
        for k in range(1, CONV_K):
            acc = acc + ext_ref[c, r0 + HIST_OFF + k:r0 + HIST_OFF + k + rb, :] * taps[k]
        y_ref[r0:r0 + rb, lanes] = acc + bias


def _causal_conv(ext_ref, dww_ref, dwb_ref, y_ref, n_rows):
    for c in range(CONV_LANE_BLOCKS):
        _causal_conv_lanes(ext_ref, dww_ref, dwb_ref, y_ref, n_rows, c)


def _ext_store(ext_ref, r0, rows_value):
    n = rows_value.shape[0]
    for c in range(CONV_LANE_BLOCKS):
        ext_ref[c, r0:r0 + n, :] = rows_value[:, c * LANES:(c + 1) * LANES]


def _ext_load(ext_ref, r0, n):
    return jnp.concatenate([ext_ref[c, r0:r0 + n, :] for c in range(CONV_LANE_BLOCKS)], axis=-1)


def _mix_out(x, ret_out, conv_out, wout_ref, g_post):
    half = wout_ref.shape[0] // 2
    mix = (_dot(conv_out.astype(_BF16), _weight(wout_ref[half:, :]))
           + _dot(ret_out.astype(_BF16), _weight(wout_ref[0:half, :])))
    return x + _rms(mix, g_post)


def _prompt_mixer_kernel(x_ref, tile_rot_ref, row_rot_ref, gpre_ref, win_ref, dmat_ref, din_ref, dout_ref,
                         sdec_ref, gng_ref, gnb_ref, dww_ref, dwb_ref, clng_ref, clnb_ref, wout_ref, gpost_ref,
                         s0_ref, c0_ref,
                         x1_ref, sfin_ref, cfin_ref,
                         s_scr, ext_scr, y_scr):
    i = pl.program_id(0)
    rows = x_ref.shape[0]

    @pl.when(i == 0)
    def _():
        s_scr[...] = s0_ref[...]
        _ext_store(ext_scr, 0, jnp.zeros((HIST_OFF, CONV_WIDTH), _F32))
        _ext_store(ext_scr, HIST_OFF, c0_ref[...])

    x = x_ref[...]
    h = _rms(x, gpre_ref[...]).astype(_BF16)

    glu = _dot(h, _weight(win_ref[:, 4 * RET_WIDTH:N_IN]))
    u = glu[:, 0:CONV_WIDTH] * jax.nn.sigmoid(glu[:, CONV_WIDTH:2 * CONV_WIDTH])
    _ext_store(ext_scr, HIST, u)
    blocks = []
    for c in range(CONV_LANE_BLOCKS):
        blocks.append(_dot(h, _weight(win_ref[:, c * RET_WIDTH:(c + 1) * RET_WIDTH])))
        _causal_conv_lanes(ext_scr, dww_ref, dwb_ref, y_scr, rows, c)
    q, k, v, gate = blocks
    ext_scr[:, 0:HIST, :] = ext_scr[:, rows:rows + HIST, :]
    conv_out = _silu(_layer_norm(y_scr[...], clng_ref[...], clnb_ref[...]))

    ret_rows = []
    for b in range(rows // RET_BLOCK):
        rs = slice(b * RET_BLOCK, (b + 1) * RET_BLOCK)
        cos_a, sin_a = tile_rot_ref[b, 0, 0:1, :], tile_rot_ref[b, 1, 0:1, :]
        cos_f = cos_a * row_rot_ref[0] - sin_a * row_rot_ref[1]
        sin_f = sin_a * row_rot_ref[2] + cos_a * row_rot_ref[3]
        qs, ks, vs = _split_heads(q[rs], k[rs], v[rs], cos_f, sin_f)
        ret_parts = []
        for hh in range(RET_HEADS):
            hl = slice(hh * HEAD_DIM, (hh + 1) * HEAD_DIM)
            qb = qs[hh].astype(_BF16)
            vb = vs[hh].astype(_BF16)
            p = (_dot_nt(qb, ks[hh].astype(_BF16)) * dmat_ref[hh]).astype(_BF16)
            intra = _dot(p, vb)
            s_old = s_scr[hh]
            inter = _dot(qb, s_old.astype(_BF16)) * din_ref[:, hl]
            kd = (ks[hh] * dout_ref[:, hl]).astype(_BF16)
            s_scr[hh] = s_old * sdec_ref[hh:hh + 1, :] + _dot_tn(kd, vb)
            ret_parts.append(_group_norm_gate(intra + inter, gate[rs, hl], gng_ref[:, hl], gnb_ref[:, hl]))
        ret_rows.append(jnp.concatenate(ret_parts, axis=-1))
    ret_out = jnp.concatenate(ret_rows, axis=0)

    x1_ref[...] = _mix_out(x, ret_out, conv_out, wout_ref, gpost_ref[...])

    @pl.when(i == pl.num_programs(0) - 1)
    def _():
        sfin_ref[...] = s_scr[...]
        cfin_ref[...] = _ext_load(ext_scr, HIST_OFF, CONV_K - 1)


def _const_spec(shape):
    zeros = (0,) * len(shape)
    return pl.BlockSpec(shape, lambda i, z=zeros: z)


def _resident_spec(shape):
    zeros = (0,) * len(shape)
    return pl.BlockSpec(shape, lambda i, z=zeros: z, pipeline_mode=pl.Buffered(1))


def _prompt_mixer(x, tile_rot, row_rot, g_pre, w_in, dmat, d_in, d_out, s_dec, gn_g, gn_b, dw_w, dw_b, cln_g, cln_b,
                  w_out, g_post, s0, c0):
    n = x.shape[0]
    t = PROMPT_TILE
    consts = (row_rot, g_pre, w_in, dmat, d_in, d_out, s_dec, gn_g, gn_b, dw_w, dw_b, cln_g, cln_b, w_out,
              g_post, s0, c0)
    big = (w_in, w_out, dmat)
    in_specs = ([pl.BlockSpec((t, D_MODEL), lambda i: (i, 0)),
                 pl.BlockSpec((t // RET_BLOCK,) + tile_rot.shape[1:], lambda i: (i, 0, 0, 0))]
                + [_resident_spec(c.shape) if any(c is b for b in big) else _const_spec(c.shape) for c in consts])
    out_shape = (jax.ShapeDtypeStruct((n, D_MODEL), _F32),
                 jax.ShapeDtypeStruct((RET_HEADS, HEAD_DIM, HEAD_DIM), _F32),
                 jax.ShapeDtypeStruct((CONV_K - 1, CONV_WIDTH), _F32))
    out_specs = (pl.BlockSpec((t, D_MODEL), lambda i: (i, 0)),
                 _const_spec((RET_HEADS, HEAD_DIM, HEAD_DIM)),
                 _const_spec((CONV_K - 1, CONV_WIDTH)))
    return pl.pallas_call(
        _prompt_mixer_kernel,
        grid=(n // t,),
        in_specs=in_specs,
        out_specs=out_specs,
        out_shape=out_shape,
        scratch_shapes=[pltpu.VMEM((RET_HEADS, HEAD_DIM, HEAD_DIM), _F32),
                        pltpu.VMEM((CONV_LANE_BLOCKS, HIST + t, LANES), _F32),
                        pltpu.VMEM((t, CONV_WIDTH), _F32)],
        compiler_params=pltpu.CompilerParams(dimension_semantics=("arbitrary",),
                                             vmem_limit_bytes=VMEM_LIMIT_BYTES),
        name="prompt_mixer",
    )(x, tile_rot, *consts)


def _sample_mixer_kernel(x_ref, meta_ref, cos_ref, sin_ref, gpre_ref, win_ref, dmat_ref, din_ref, dout_ref, sdec_ref,
                         gng_ref, gnb_ref, dww_ref, dwb_ref, clng_ref, clnb_ref, wout_ref, gpost_ref,
                         state_ref, cache_ref,
                         x1_ref, snew_ref, cnew_ref, smeta_ref, cmeta_ref, win_packed_ref, wout_packed_ref,
                         ext_scr, y_scr):
    n_streams = state_ref.shape[0]
    stream_rows = x_ref.shape[0]

    @pl.when(pl.program_id(0) == 0)
    def _():
        win_packed_ref[...] = pltpu.bitcast(win_ref[...].astype(_BF16), jnp.uint32)
        wout_packed_ref[...] = pltpu.bitcast(wout_ref[...].astype(_BF16), jnp.uint32)

    wout_ref = wout_packed_ref
    x = jnp.concatenate([x_ref[...], meta_ref[...]], axis=0)
    qs, ks, vs, gate, u = _project(x, gpre_ref[...], _weight(win_packed_ref[...]), cos_ref[...], sin_ref[...])

    ret_parts = []
    for h in range(RET_HEADS):
        hl = slice(h * HEAD_DIM, (h + 1) * HEAD_DIM)
        qb = qs[h].astype(_BF16)
        vb = vs[h].astype(_BF16)
        p = (_dot_nt(qb, ks[h].astype(_BF16)) * dmat_ref[h]).astype(_BF16)
        intra = _dot(p, vb)
        kd = (ks[h] * dout_ref[:, hl]).astype(_BF16)
        inter_parts = []
        for s in range(n_streams):
            rs = slice(s * STREAM_ROWS, (s + 1) * STREAM_ROWS)
            s_old = state_ref[s, h]
            inter_parts.append(_dot(qb[rs], s_old.astype(_BF16)))
            snew_ref[s, h] = s_old * sdec_ref[h:h + 1, :] + _dot_tn(kd[rs], vb[rs])
        ms = slice(stream_rows, stream_rows + STREAM_ROWS)
        inter_parts.append(jnp.zeros((STREAM_ROWS, HEAD_DIM), _F32))
        smeta_ref[h] = _dot_tn(kd[ms], vb[ms])
        inter = jnp.concatenate(inter_parts, axis=0) * din_ref[:, hl]
        ret_parts.append(_group_norm_gate(intra + inter, _head(gate, h), gng_ref[:, hl], gnb_ref[:, hl]))
    ret_out = jnp.concatenate(ret_parts, axis=-1)

    tail = slice(HIST + STREAM_ROWS - (CONV_K - 1), HIST + STREAM_ROWS)
    for s in range(n_streams + 1):
        rs = slice(s * STREAM_ROWS, (s + 1) * STREAM_ROWS)
        _ext_store(ext_scr, 0, jnp.zeros((HIST_OFF, CONV_WIDTH), _F32))
        if s < n_streams:
            _ext_store(ext_scr, HIST_OFF, cache_ref[s])
        else:
            _ext_store(ext_scr, HIST_OFF, jnp.zeros((CONV_K - 1, CONV_WIDTH), _F32))
        _ext_store(ext_scr, HIST, u[rs])
        _causal_conv(ext_scr, dww_ref, dwb_ref, y_scr.at[rs], STREAM_ROWS)
        new_cache = _ext_load(ext_scr, tail.start, CONV_K - 1)
        if s < n_streams:
            cnew_ref[s] = new_cache
        else:
            cmeta_ref[...] = new_cache
    conv_out = _silu(_layer_norm(y_scr[...], clng_ref[...], clnb_ref[...]))

    x1_ref[...] = _mix_out(x, ret_out, conv_out, wout_ref, gpost_ref[...])[0:stream_rows]


def _sample_mixer(x_rows, meta, cos_f, sin_f, g_pre, w_in, dmat, d_in, d_out, s_dec, gn_g, gn_b, dw_w, dw_b, cln_g,
                  cln_b, w_out, g_post, state, cache):
    n = x_rows.shape[0]
    n_streams = state.shape[0]
    group = SAMPLE_GROUP_STREAMS
    rows = group * STREAM_ROWS
    consts = (meta, cos_f, sin_f, g_pre, w_in, dmat, d_in, d_out, s_dec, gn_g, gn_b, dw_w, dw_b, cln_g, cln_b, w_out,
              g_post)
    big = (w_in, w_out)
    in_specs = ([pl.BlockSpec((rows, D_MODEL), lambda i: (i, 0))]
                + [_resident_spec(c.shape) if any(c is b for b in big) else _const_spec(c.shape) for c in consts]
                + [pl.BlockSpec((group, RET_HEADS, HEAD_DIM, HEAD_DIM), lambda i: (i, 0, 0, 0)),
                   pl.BlockSpec((group, CONV_K - 1, CONV_WIDTH), lambda i: (i, 0, 0))])
    packed = [(w.shape[0] // 2, w.shape[1]) for w in (w_in, w_out)]
    out_shape = (jax.ShapeDtypeStruct((n, D_MODEL), _F32),
                 jax.ShapeDtypeStruct((n_streams, RET_HEADS, HEAD_DIM, HEAD_DIM), _F32),
                 jax.ShapeDtypeStruct((n_streams, CONV_K - 1, CONV_WIDTH), _F32),
                 jax.ShapeDtypeStruct((RET_HEADS, HEAD_DIM, HEAD_DIM), _F32),
                 jax.ShapeDtypeStruct((CONV_K - 1, CONV_WIDTH), _F32),
                 jax.ShapeDtypeStruct(packed[0], jnp.uint32),
                 jax.ShapeDtypeStruct(packed[1], jnp.uint32))
    out_specs = (pl.BlockSpec((rows, D_MODEL), lambda i: (i, 0)),
                 pl.BlockSpec((group, RET_HEADS, HEAD_DIM, HEAD_DIM), lambda i: (i, 0, 0, 0)),
                 pl.BlockSpec((group, CONV_K - 1, CONV_WIDTH), lambda i: (i, 0, 0)),
                 _const_spec((RET_HEADS, HEAD_DIM, HEAD_DIM)),
                 _const_spec((CONV_K - 1, CONV_WIDTH)),
                 _const_spec(packed[0]),
                 _const_spec(packed[1]))
    return pl.pallas_call(
        _sample_mixer_kernel,
        grid=(n_streams // group,),
        in_specs=in_specs,
        out_specs=out_specs,
        out_shape=out_shape,
        scratch_shapes=[pltpu.VMEM((CONV_LANE_BLOCKS, HIST + STREAM_ROWS, LANES), _F32),
                        pltpu.VMEM((rows + STREAM_ROWS, CONV_WIDTH), _F32)],
        compiler_params=pltpu.CompilerParams(dimension_semantics=("arbitrary",),
                                             vmem_limit_bytes=VMEM_LIMIT_BYTES),
        name="sample_mixer",
    )(x_rows, *consts, state, cache)


def _mlp_rows(x, g_pre, w1_ref, w2_ref, g_post):
    h = _rms(x, g_pre).astype(_BF16)
    a = jnp.maximum(_dot(h, _weight(w1_ref[...])), 0.0)
    f = _dot((a * a).astype(_BF16), _weight(w2_ref[...]))
    return x + _rms(f, g_post)


def _mlp_kernel(xa_ref, xb_ref, gpre_ref, w1_ref, w2_ref, gpost_ref, ya_ref, yb_ref, *, a_tiles):
    i = pl.program_id(0)

    @pl.when(i < a_tiles)
    def _():
        ya_ref[...] = _mlp_rows(xa_ref[...], gpre_ref[...], w1_ref, w2_ref, gpost_ref[...])

    @pl.when(i >= a_tiles)
    def _():
        yb_ref[...] = _mlp_rows(xb_ref[...], gpre_ref[...], w1_ref, w2_ref, gpost_ref[...])


def _mlp(xa, xb, g_pre, w1, w2, g_post):
    t = MLP_TILE
    a_tiles, b_tiles = xa.shape[0] // t, xb.shape[0] // t
    a_blk = lambda i: (jnp.minimum(i, a_tiles - 1), 0)
    b_blk = lambda i: (jnp.maximum(i - a_tiles, 0), 0)
    return pl.pallas_call(
        functools.partial(_mlp_kernel, a_tiles=a_tiles),
        grid=(a_tiles + b_tiles,),
        in_specs=[pl.BlockSpec((t, D_MODEL), a_blk),
                  pl.BlockSpec((t, D_MODEL), b_blk),
                  _const_spec(g_pre.shape),
                  _resident_spec(w1.shape),
                  _resident_spec(w2.shape),
                  _const_spec(g_post.shape)],
        out_specs=(pl.BlockSpec((t, D_MODEL), a_blk), pl.BlockSpec((t, D_MODEL), b_blk)),
        out_shape=(jax.ShapeDtypeStruct(xa.shape, _F32), jax.ShapeDtypeStruct(xb.shape, _F32)),
        compiler_params=pltpu.CompilerParams(dimension_semantics=("arbitrary",),
                                             vmem_limit_bytes=VMEM_LIMIT_BYTES),
        name="mlp",
    )(xa, xb, g_pre, w1, w2, g_post)


def _pack_kernel(w_ref, o_ref):
    o_ref[...] = pltpu.bitcast(w_ref[...].astype(_BF16), jnp.uint32)


def _pack_weight(w):
    k, n = w.shape
    tk = k
    while tk * n * 4 > PACK_BLOCK_BYTES and tk % (4 * SUBLANES) == 0:
        tk //= 2
    return pl.pallas_call(
        _pack_kernel,
        grid=(k // tk,),
        in_specs=[pl.BlockSpec((tk, n), lambda i: (i, 0))],
        out_specs=pl.BlockSpec((tk // 2, n), lambda i: (i, 0)),
        out_shape=jax.ShapeDtypeStruct((k // 2, n), jnp.uint32),
        compiler_params=pltpu.CompilerParams(dimension_semantics=("arbitrary",),
                                             vmem_limit_bytes=VMEM_LIMIT_BYTES),
        name="pack_weight",
    )(w)


def _log_gamma():
    return jnp.log1p(-jnp.exp2(-5.0 - jnp.arange(RET_HEADS, dtype=_F32)))


def _rope_tables(pos):
    half = HEAD_DIM // 2
    inv = ROPE_BASE ** (-jnp.arange(half, dtype=_F32) / half)
    ang = pos.astype(_F32)[:, None] * inv[None, :]
    cos = jnp.cos(ang)
    sin = jnp.sin(ang)
    return jnp.concatenate([cos, cos], axis=-1), jnp.concatenate([-sin, sin], axis=-1)


def _prompt_rope_tables(n_tiles, tile):
    half = HEAD_DIM // 2
    inv = ROPE_BASE ** (-jnp.arange(half, dtype=_F32) / half)
    base = (N_META + tile * jnp.arange(n_tiles, dtype=jnp.int32)).astype(_F32)[:, None] * inv[None, :]
    off = jnp.arange(tile, dtype=jnp.int32).astype(_F32)[:, None] * inv[None, :]
    dup = lambda a: jnp.concatenate([a, a], axis=-1)
    sign = jnp.concatenate([-jnp.ones((half,), _F32), jnp.ones((half,), _F32)])
    tile_rot = jnp.stack([dup(jnp.cos(base)), dup(jnp.sin(base))], axis=1)
    tile_rot = jnp.broadcast_to(tile_rot[:, :, None, :], (n_tiles, 2, SUBLANES, HEAD_DIM))
    cos_o, sin_o = dup(jnp.cos(off)), dup(jnp.sin(off))
    row_rot = jnp.stack([cos_o, sin_o, sign * cos_o, sign * sin_o])
    return tile_rot, row_rot


def _lanes(per_head):
    return jnp.repeat(per_head, HEAD_DIM, axis=1)


def _decay_tables(rows, segment, causal_chunk):
    lg = _log_gamma()
    idx = jnp.arange(rows)
    loc = (idx % segment).astype(_F32)
    same_seg = (idx[:, None] // segment) == (idx[None, :] // segment)
    visible = same_seg & ((idx[None, :] // causal_chunk) <= (idx[:, None] // causal_chunk))
    dist = jnp.abs(idx[:, None] - idx[None, :]).astype(_F32)
    dmat = jnp.where(visible[None], jnp.exp(lg[:, None, None] * dist[None]), 0.0)
    d_in = _lanes(jnp.exp(lg[None, :] * (loc[:, None] + 1.0)))
    d_out = _lanes(jnp.exp(lg[None, :] * (segment - 1.0 - loc[:, None])))
    s_dec = jnp.broadcast_to(jnp.exp(lg * segment)[:, None], (RET_HEADS, HEAD_DIM))
    return dmat, d_in, d_out, s_dec


def kernel(x_prompt, x_sample, state_ret, cache_conv, meta, g_pre_mix, w_in, gn_g, gn_b, dw_w, dw_b, cln_g, cln_b,
           w_out, g_post_mix, g_pre_mlp, w_mlp_in, w_mlp_out, g_post_mlp):
    batch, seq, _ = x_prompt.shape
    dec_batch, dec_seq, _ = x_sample.shape
    assert batch == 1 and dec_seq == STREAM_ROWS and meta.shape[0] == N_META == STREAM_ROWS
    assert seq % PROMPT_TILE == 0 and seq % MLP_TILE == 0 and dec_batch % SAMPLE_GROUP_STREAMS == 0
    assert (dec_batch * dec_seq) % MLP_TILE == 0 and state_ret.shape[0] == 1

    l = 0
    post = (gn_g[l][None], gn_b[l][None], dw_w[l], dw_b[l][None], cln_g[l][None], cln_b[l][None])
    mlp_w = (g_pre_mlp[l][None], _pack_weight(w_mlp_in[l]), _pack_weight(w_mlp_out[l]), g_post_mlp[l][None])

    group_rows = (SAMPLE_GROUP_STREAMS + 1) * STREAM_ROWS
    pos_stream = N_META + PAST_LEN + jnp.arange(dec_seq, dtype=jnp.int32)
    pos_group = jnp.concatenate([jnp.tile(pos_stream, SAMPLE_GROUP_STREAMS), jnp.arange(N_META, dtype=jnp.int32)])
    cos_s, sin_s = _rope_tables(pos_group)
    dec_s = _decay_tables(group_rows, STREAM_ROWS, STREAM_ROWS)
    xs1, s_s, c_s, s_meta, c_meta, w_in_packed, w_out_packed = _sample_mixer(
        x_sample.reshape(dec_batch * dec_seq, D_MODEL), meta, cos_s, sin_s, g_pre_mix[l][None], w_in[l], *dec_s,
        *post, w_out[l], g_post_mix[l][None], state_ret[l], cache_conv[l])

    rot_p = _prompt_rope_tables(seq // RET_BLOCK, RET_BLOCK)
    dec_p = _decay_tables(RET_BLOCK, RET_BLOCK, CHUNK)
    xp1, s_p, c_p = _prompt_mixer(x_prompt[0], *rot_p, g_pre_mix[l][None], w_in_packed, *dec_p, *post, w_out_packed,
                                  g_post_mix[l][None], s_meta, c_meta)
    yp, ys = _mlp(xp1, xs1, *mlp_w)

    return (yp[None], ys.reshape(dec_batch, dec_seq, D_MODEL), s_p[None, None], c_p[None, None],
            s_s[None], c_s[None])
```

```python
import functools

import jax
import jax.numpy as jnp
import numpy as np
from jax import lax
from jax.experimental import pallas as pl
from jax.experimental.pallas import tpu as pltpu

D_MODEL = 1024
CHUNK = 64
N_META = 16
PAST_LEN = 4096
RET_WIDTH = 512
RET_HEADS = 4
HEAD_DIM = 128
CONV_WIDTH = 512
CONV_K = 31
D_FF = 4096
N_IN = 4 * RET_WIDTH + 2 * CONV_WIDTH
EPS = 1e-6
ROPE_BASE = 10000.0

LANES = 128
SUBLANES = 8
HIST = 32
HIST_OFF = HIST - (CONV_K - 1)
RET_BLOCK = 256
PROMPT_TILE = 1024
STREAM_ROWS = 16
SAMPLE_GROUP_STREAMS = 8
MLP_TILE = 512
PACK_BLOCK_BYTES = 4 * 1024 * 1024
CONV_ROW_BLOCK = 32
CONV_LANE_BLOCKS = CONV_WIDTH // LANES
VMEM_LIMIT_BYTES = 56 * 1024 * 1024

_F32 = jnp.float32
_BF16 = jnp.bfloat16


def _rms(x, g):
    return x * lax.rsqrt(jnp.mean(x * x, axis=-1, keepdims=True) + EPS) * g


def _silu(x):
    return x * jax.nn.sigmoid(x)


def _dot(a, b):
    return jnp.dot(a, b, preferred_element_type=_F32)


def _dot_nt(a, b):
    return lax.dot_general(a, b, (((1,), (1,)), ((), ())), preferred_element_type=_F32)


def _dot_tn(a, b):
    return lax.dot_general(a, b, (((0,), (0,)), ((), ())), preferred_element_type=_F32)


def _weight(packed):
    return pltpu.bitcast(packed, _BF16)


def _head(x, h):
    return x[:, h * HEAD_DIM:(h + 1) * HEAD_DIM]


def _rope(xh, cos_f, sin_f):
    return xh * cos_f + pltpu.roll(xh, HEAD_DIM // 2, 1) * sin_f


def _split_heads(q, k, v, cos_f, sin_f):
    qs = [_rope(_head(q, hh), cos_f, sin_f) for hh in range(RET_HEADS)]
    ks = [_rope(_head(k, hh), cos_f, sin_f) * (HEAD_DIM ** -0.5) for hh in range(RET_HEADS)]
    vs = [_head(v, hh) for hh in range(RET_HEADS)]
    return qs, ks, vs


def _project(x, g_pre, w_in, cos_f, sin_f):
    h = _rms(x, g_pre).astype(_BF16)
    proj = _dot(h, w_in)
    q = proj[:, 0:RET_WIDTH]
    k = proj[:, RET_WIDTH:2 * RET_WIDTH]
    v = proj[:, 2 * RET_WIDTH:3 * RET_WIDTH]
    gate = proj[:, 3 * RET_WIDTH:4 * RET_WIDTH]
    ga = proj[:, 4 * RET_WIDTH:4 * RET_WIDTH + CONV_WIDTH]
    gb = proj[:, 4 * RET_WIDTH + CONV_WIDTH:N_IN]
    qs, ks, vs = _split_heads(q, k, v, cos_f, sin_f)
    return qs, ks, vs, gate, ga * jax.nn.sigmoid(gb)


def _group_norm_gate(ret_h, gate_h, g, b):
    mu = jnp.mean(ret_h, axis=-1, keepdims=True)
    d = ret_h - mu
    var = jnp.mean(d * d, axis=-1, keepdims=True)
    rn = d * lax.rsqrt(var + EPS) * g + b
    return rn * _silu(gate_h)


def _layer_norm(x, g, b):
    mu = jnp.mean(x, axis=-1, keepdims=True)
    d = x - mu
    var = jnp.mean(d * d, axis=-1, keepdims=True)
    return d * lax.rsqrt(var + EPS) * g + b


def _causal_conv_lanes(ext_ref, dww_ref, dwb_ref, y_ref, n_rows, c):
    rb = min(CONV_ROW_BLOCK, n_rows)
    lanes = slice(c * LANES, (c + 1) * LANES)
    taps = [dww_ref[k:k + 1, lanes] for k in range(CONV_K)]
    bias = dwb_ref[:, lanes]
    for r0 in range(0, n_rows, rb):
        acc = ext_ref[c, r0 + HIST_OFF:r0 + HIST_OFF + rb, :] * taps[0]
        for k in range(1, CONV_K):
            acc = acc + ext_ref[c, r0 + HIST_OFF + k:r0 + HIST_OFF + k + rb, :] * taps[k]
        y_ref[r0:r0 + rb, lanes] = acc + bias


def _causal_conv(ext_ref, dww_ref, dwb_ref, y_ref, n_rows):
    for c in range(CONV_LANE_BLOCKS):
        _causal_conv_lanes(ext_ref, dww_ref, dwb_ref, y_ref, n_rows, c)


def _ext_store(ext_ref, r0, rows_value):
    n = rows_value.shape[0]
    for c in range(CONV_LANE_BLOCKS):
        ext_ref[c, r0:r0 + n, :] = rows_value[:, c * LANES:(c + 1) * LANES]


def _ext_load(ext_ref, r0, n):
    return jnp.concatenate([ext_ref[c, r0:r0 + n, :] for c in range(CONV_LANE_BLOCKS)], axis=-1)


def _mix_out(x, ret_out, conv_out, wout_ref, g_post):
    half = wout_ref.shape[0] // 2
    mix = (_dot(conv_out.astype(_BF16), _weight(wout_ref[half:, :]))
           + _dot(ret_out.astype(_BF16), _weight(wout_ref[0:half, :])))
    return x + _rms(mix, g_post)


def _prompt_mixer_kernel(x_ref, tile_rot_ref, row_rot_ref, gpre_ref, win_ref, dmat_ref, din_ref, dout_ref,
                         sdec_ref, gng_ref, gnb_ref, dww_ref, dwb_ref, clng_ref, clnb_ref, wout_ref, gpost_ref,
                         s0_ref, c0_ref,
                         x1_ref, sfin_ref, cfin_ref,
                         s_scr, ext_scr, y_scr):
    i = pl.program_id(0)
    rows = x_ref.shape[0]

    @pl.when(i == 0)
    def _():
        s_scr[...] = s0_ref[...]
        _ext_store(ext_scr, 0, jnp.zeros((HIST_OFF, CONV_WIDTH), _F32))
        _ext_store(ext_scr, HIST_OFF, c0_ref[...])

    x = x_ref[...]
    h = _rms(x, gpre_ref[...]).astype(_BF16)

    glu = _dot(h, _weight(win_ref[:, 4 * RET_WIDTH:N_IN]))
    u = glu[:, 0:CONV_WIDTH] * jax.nn.sigmoid(glu[:, CONV_WIDTH:2 * CONV_WIDTH])
    _ext_store(ext_scr, HIST, u)
    blocks = []
    for c in range(CONV_LANE_BLOCKS):
        blocks.append(_dot(h, _weight(win_ref[:, c * RET_WIDTH:(c + 1) * RET_WIDTH])))
        _causal_conv_lanes(ext_scr, dww_ref, dwb_ref, y_scr, rows, c)
    q, k, v, gate = blocks
    ext_scr[:, 0:HIST, :] = ext_scr[:, rows:rows + HIST, :]
    conv_out = _silu(_layer_norm(y_scr[...], clng_ref[...], clnb_ref[...]))

    ret_rows = []
    for b in range(rows // RET_BLOCK):
        rs = slice(b * RET_BLOCK, (b + 1) * RET_BLOCK)
        cos_a, sin_a = tile_rot_ref[b, 0, 0:1, :], tile_rot_ref[b, 1, 0:1, :]
        cos_f = cos_a * row_rot_ref[0] - sin_a * row_rot_ref[1]
        sin_f = sin_a * row_rot_ref[2] + cos_a * row_rot_ref[3]
        qs, ks, vs = _split_heads(q[rs], k[rs], v[rs], cos_f, sin_f)
        ret_parts = []
        for hh in range(RET_HEADS):
            hl = slice(hh * HEAD_DIM, (hh + 1) * HEAD_DIM)
            qb = qs[hh].astype(_BF16)
            vb = vs[hh].astype(_BF16)
            p = (_dot_nt(qb, ks[hh].astype(_BF16)) * dmat_ref[hh]).astype(_BF16)
            intra = _dot(p, vb)
            s_old = s_scr[hh]
            inter = _dot(qb, s_old.astype(_BF16)) * din_ref[:, hl]
            kd = (ks[hh] * dout_ref[:, hl]).astype(_BF16)
            s_scr[hh] = s_old * sdec_ref[hh:hh + 1, :] + _dot_tn(kd, vb)
            ret_parts.append(_group_norm_gate(intra + inter, gate[rs, hl], gng_ref[:, hl], gnb_ref[:, hl]))
        ret_rows.append(jnp.concatenate(ret_parts, axis=-1))
    ret_out = jnp.concatenate(ret_rows, axis=0)

    x1_ref[...] = _mix_out(x, ret_out, conv_out, wout_ref, gpost_ref[...])

    @pl.when(i == pl.num_programs(0) - 1)
    def _():
        sfin_ref[...] = s_scr[...]
        cfin_ref[...] = _ext_load(ext_scr, HIST_OFF, CONV_K - 1)


def _const_spec(shape):
    zeros = (0,) * len(shape)
    return pl.BlockSpec(shape, lambda i, z=zeros: z)


def _resident_spec(shape):
    zeros = (0,) * len(shape)
    return pl.BlockSpec(shape, lambda i, z=zeros: z, pipeline_mode=pl.Buffered(1))


def _prompt_mixer(x, tile_rot, row_rot, g_pre, w_in, dmat, d_in, d_out, s_dec, gn_g, gn_b, dw_w, dw_b, cln_g, cln_b,
                  w_out, g_post, s0, c0):
    n = x.shape[0]
    t = PROMPT_TILE
    consts = (row_rot, g_pre, w_in, dmat, d_in, d_out, s_dec, gn_g, gn_b, dw_w, dw_b, cln_g, cln_b, w_out,
              g_post, s0, c0)
    big = (w_in, w_out, dmat)
    in_specs = ([pl.BlockSpec((t, D_MODEL), lambda i: (i, 0)),
                 pl.BlockSpec((t // RET_BLOCK,) + tile_rot.shape[1:], lambda i: (i, 0, 0, 0))]
                + [_resident_spec(c.shape) if any(c is b for b in big) else _const_spec(c.shape) for c in consts])
    out_shape = (jax.ShapeDtypeStruct((n, D_MODEL), _F32),
                 jax.ShapeDtypeStruct((RET_HEADS, HEAD_DIM, HEAD_DIM), _F32),
                 jax.ShapeDtypeStruct((CONV_K - 1, CONV_WIDTH), _F32))
    out_specs = (pl.BlockSpec((t, D_MODEL), lambda i: (i, 0)),
                 _const_spec((RET_HEADS, HEAD_DIM, HEAD_DIM)),
                 _const_spec((CONV_K - 1, CONV_WIDTH)))
    return pl.pallas_call(
        _prompt_mixer_kernel,
        grid=(n // t,),
        in_specs=in_specs,
        out_specs=out_specs,
        out_shape=out_shape,
        scratch_shapes=[pltpu.VMEM((RET_HEADS, HEAD_DIM, HEAD_DIM), _F32),
                        pltpu.VMEM((CONV_LANE_BLOCKS, HIST + t, LANES), _F32),
                        pltpu.VMEM((t, CONV_WIDTH), _F32)],
        compiler_params=pltpu.CompilerParams(dimension_semantics=("arbitrary",),
                                             vmem_limit_bytes=VMEM_LIMIT_BYTES),
        name="prompt_mixer",
    )(x, tile_rot, *consts)


def _sample_mixer_kernel(x_ref, meta_ref, cos_ref, sin_ref, gpre_ref, win_ref, dmat_ref, din_ref, dout_ref, sdec_ref,
                         gng_ref, gnb_ref, dww_ref, dwb_ref, clng_ref, clnb_ref, wout_ref, gpost_ref,
                         state_ref, cache_ref,
                         x1_ref, snew_ref, cnew_ref, smeta_ref, cmeta_ref, win_packed_ref, wout_packed_ref,
                         ext_scr, y_scr, extt_scr, uslab_scr, yslab_scr):
    n_streams = state_ref.shape[0]
    stream_rows = x_ref.shape[0]

    @pl.when(pl.program_id(0) == 0)
    def _():
        win_packed_ref[...] = pltpu.bitcast(win_ref[...].astype(_BF16), jnp.uint32)
        wout_packed_ref[...] = pltpu.bitcast(wout_ref[...].astype(_BF16), jnp.uint32)

    wout_ref = wout_packed_ref
    x = jnp.concatenate([x_ref[...], meta_ref[...]], axis=0)
    qs, ks, vs, gate, u = _project(x, gpre_ref[...], _weight(win_packed_ref[...]), cos_ref[...], sin_ref[...])

    ret_parts = []
    for h in range(RET_HEADS):
        hl = slice(h * HEAD_DIM, (h + 1) * HEAD_DIM)
        qb = qs[h].astype(_BF16)
        vb = vs[h].astype(_BF16)
        p = (_dot_nt(qb, ks[h].astype(_BF16)) * dmat_ref[h]).astype(_BF16)
        intra = _dot(p, vb)
        kd = (ks[h] * dout_ref[:, hl]).astype(_BF16)
        inter_parts = []
        for s in range(n_streams):
            rs = slice(s * STREAM_ROWS, (s + 1) * STREAM_ROWS)
            s_old = state_ref[s, h]
            inter_parts.append(_dot(qb[rs], s_old.astype(_BF16)))
            snew_ref[s, h] = s_old * sdec_ref[h:h + 1, :] + _dot_tn(kd[rs], vb[rs])
        ms = slice(stream_rows, stream_rows + STREAM_ROWS)
        inter_parts.append(jnp.zeros((STREAM_ROWS, HEAD_DIM), _F32))
        smeta_ref[h] = _dot_tn(kd[ms], vb[ms])
        inter = jnp.concatenate(inter_parts, axis=0) * din_ref[:, hl]
        ret_parts.append(_group_norm_gate(intra + inter, _head(gate, h), gng_ref[:, hl], gnb_ref[:, hl]))
    ret_out = jnp.concatenate(ret_parts, axis=-1)

    hist = CONV_K - 1
    for c in range(CONV_LANE_BLOCKS):
        uslab_scr[c] = u[0:stream_rows, c * LANES:(c + 1) * LANES]
    extt_scr[0:hist] = cache_ref[...]
    for t in range(STREAM_ROWS):
        extt_scr[hist + t] = jnp.concatenate(
            [uslab_scr[c, pl.ds(t, n_streams, stride=STREAM_ROWS), :] for c in range(CONV_LANE_BLOCKS)], axis=-1)
    for t in range(STREAM_ROWS):
        acc = extt_scr[t] * dww_ref[0:1, :]
        for k in range(1, CONV_K):
            acc = acc + extt_scr[t + k] * dww_ref[k:k + 1, :]
        y_t = acc + dwb_ref[...]
        for c in range(CONV_LANE_BLOCKS):
            yslab_scr[c, pl.ds(t, n_streams, stride=STREAM_ROWS), :] = y_t[:, c * LANES:(c + 1) * LANES]
    cnew_ref[...] = extt_scr[STREAM_ROWS:STREAM_ROWS + hist]

    _ext_store(ext_scr, 0, jnp.zeros((HIST, CONV_WIDTH), _F32))
    _ext_store(ext_scr, HIST, u[stream_rows:stream_rows + STREAM_ROWS])
    _causal_conv(ext_scr, dww_ref, dwb_ref, y_scr, STREAM_ROWS)
    cmeta_ref[...] = _ext_load(ext_scr, HIST + STREAM_ROWS - hist, hist)

    y_streams = jnp.concatenate([yslab_scr[c] for c in range(CONV_LANE_BLOCKS)], axis=-1)
    y_all = jnp.concatenate([y_streams, y_scr[...]], axis=0)
    conv_out = _silu(_layer_norm(y_all, clng_ref[...], clnb_ref[...]))

    x1_ref[...] = _mix_out(x, ret_out, conv_out, wout_ref, gpost_ref[...])[0:stream_rows]


def _sample_mixer(x_rows, meta, cos_f, sin_f, g_pre, w_in, dmat, d_in, d_out, s_dec, gn_g, gn_b, dw_w, dw_b, cln_g,
                  cln_b, w_out, g_post, state, cache):
    n = x_rows.shape[0]
    n_streams = state.shape[0]
    group = SAMPLE_GROUP_STREAMS
    rows = group * STREAM_ROWS
    consts = (meta, cos_f, sin_f, g_pre, w_in, dmat, d_in, d_out, s_dec, gn_g, gn_b, dw_w, dw_b, cln_g, cln_b, w_out,
              g_post)
    big = (w_in, w_out)
    assert group == SUBLANES
    cache_blk = pl.BlockSpec((CONV_K - 1, group, CONV_WIDTH), lambda i: (0, i, 0))
    in_specs = ([pl.BlockSpec((rows, D_MODEL), lambda i: (i, 0))]
                + [_resident_spec(c.shape) if any(c is b for b in big) else _const_spec(c.shape) for c in consts]
                + [pl.BlockSpec((group, RET_HEADS, HEAD_DIM, HEAD_DIM), lambda i: (i, 0, 0, 0)), cache_blk])
    packed = [(w.shape[0] // 2, w.shape[1]) for w in (w_in, w_out)]
    out_shape = (jax.ShapeDtypeStruct((n, D_MODEL), _F32),
                 jax.ShapeDtypeStruct((n_streams, RET_HEADS, HEAD_DIM, HEAD_DIM), _F32),
                 jax.ShapeDtypeStruct((CONV_K - 1, n_streams, CONV_WIDTH), _F32),
                 jax.ShapeDtypeStruct((RET_HEADS, HEAD_DIM, HEAD_DIM), _F32),
                 jax.ShapeDtypeStruct((CONV_K - 1, CONV_WIDTH), _F32),
                 jax.ShapeDtypeStruct(packed[0], jnp.uint32),
                 jax.ShapeDtypeStruct(packed[1], jnp.uint32))
    out_specs = (pl.BlockSpec((rows, D_MODEL), lambda i: (i, 0)),
                 pl.BlockSpec((group, RET_HEADS, HEAD_DIM, HEAD_DIM), lambda i: (i, 0, 0, 0)),
                 cache_blk,
                 _const_spec((RET_HEADS, HEAD_DIM, HEAD_DIM)),
                 _const_spec((CONV_K - 1, CONV_WIDTH)),
                 _const_spec(packed[0]),
                 _const_spec(packed[1]))
    return pl.pallas_call(
        _sample_mixer_kernel,
        grid=(n_streams // group,),
        in_specs=in_specs,
        out_specs=out_specs,
        out_shape=out_shape,
        scratch_shapes=[pltpu.VMEM((CONV_LANE_BLOCKS, HIST + STREAM_ROWS, LANES), _F32),
                        pltpu.VMEM((STREAM_ROWS, CONV_WIDTH), _F32),
                        pltpu.VMEM((CONV_K - 1 + STREAM_ROWS, group, CONV_WIDTH), _F32),
                        pltpu.VMEM((CONV_LANE_BLOCKS, rows, LANES), _F32),
                        pltpu.VMEM((CONV_LANE_BLOCKS, rows, LANES), _F32)],
        compiler_params=pltpu.CompilerParams(dimension_semantics=("arbitrary",),
                                             vmem_limit_bytes=VMEM_LIMIT_BYTES),
        name="sample_mixer",
    )(x_rows, *consts, state, cache)


def _mlp_rows(x, g_pre, w1_ref, w2_ref, g_post):
    h = _rms(x, g_pre).astype(_BF16)
    a = jnp.maximum(_dot(h, _weight(w1_ref[...])), 0.0)
    f = _dot((a * a).astype(_BF16), _weight(w2_ref[...]))
    return x + _rms(f, g_post)


def _mlp_kernel(xa_ref, xb_ref, gpre_ref, w1_ref, w2_ref, gpost_ref, ya_ref, yb_ref, *, a_tiles):
    i = pl.program_id(0)

    @pl.when(i < a_tiles)
    def _():
        ya_ref[...] = _mlp_rows(xa_ref[...], gpre_ref[...], w1_ref, w2_ref, gpost_ref[...])

    @pl.when(i >= a_tiles)
    def _():
        yb_ref[...] = _mlp_rows(xb_ref[...], gpre_ref[...], w1_ref, w2_ref, gpost_ref[...])


def _mlp(xa, xb, g_pre, w1, w2, g_post):
    t = MLP_TILE
    a_tiles, b_tiles = xa.shape[0] // t, xb.shape[0] // t
    a_blk = lambda i: (jnp.minimum(i, a_tiles - 1), 0)
    b_blk = lambda i: (jnp.maximum(i - a_tiles, 0), 0)
    return pl.pallas_call(
        functools.partial(_mlp_kernel, a_tiles=a_tiles),
        grid=(a_tiles + b_tiles,),
        in_specs=[pl.BlockSpec((t, D_MODEL), a_blk),
                  pl.BlockSpec((t, D_MODEL), b_blk),
                  _const_spec(g_pre.shape),
                  _resident_spec(w1.shape),
                  _resident_spec(w2.shape),
                  _const_spec(g_post.shape)],
        out_specs=(pl.BlockSpec((t, D_MODEL), a_blk), pl.BlockSpec((t, D_MODEL), b_blk)),
        out_shape=(jax.ShapeDtypeStruct(xa.shape, _F32), jax.ShapeDtypeStruct(xb.shape, _F32)),
        compiler_params=pltpu.CompilerParams(dimension_semantics=("arbitrary",),
                                             vmem_limit_bytes=VMEM_LIMIT_BYTES),
        name="mlp",
    )(xa, xb, g_pre, w1, w2, g_post)


def _pack_kernel(w_ref, o_ref):
    o_ref[...] = pltpu.bitcast(w_ref[...].astype(_BF16), jnp.uint32)


def _pack_weight(w):
    k, n = w.shape
    tk = k
    while tk * n * 4 > PACK_BLOCK_BYTES and tk % (4 * SUBLANES) == 0:
        tk //= 2
    return pl.pallas_call(
        _pack_kernel,
        grid=(k // tk,),
        in_specs=[pl.BlockSpec((tk, n), lambda i: (i, 0))],
        out_specs=pl.BlockSpec((tk // 2, n), lambda i: (i, 0)),
        out_shape=jax.ShapeDtypeStruct((k // 2, n), jnp.uint32),
        compiler_params=pltpu.CompilerParams(dimension_semantics=("arbitrary",),
                                             vmem_limit_bytes=VMEM_LIMIT_BYTES),
        name="pack_weight",
    )(w)


def _f32(a):
    return np.ascontiguousarray(a, dtype=np.float32)


def _log_gamma():
    return np.log1p(-np.exp2(-5.0 - np.arange(RET_HEADS, dtype=np.float64)))


def _rope_freqs():
    half = HEAD_DIM // 2
    return ROPE_BASE ** (-np.arange(half, dtype=np.float64) / half)


def _rope_tables(pos):
    ang = pos.astype(np.float64)[:, None] * _rope_freqs()[None, :]
    cos = np.cos(ang)
    sin = np.sin(ang)
    return _f32(np.concatenate([cos, cos], axis=-1)), _f32(np.concatenate([-sin, sin], axis=-1))


def _prompt_rope_tables(n_tiles, tile):
    half = HEAD_DIM // 2
    inv = _rope_freqs()
    base = (N_META + tile * np.arange(n_tiles)).astype(np.float64)[:, None] * inv[None, :]
    off = np.arange(tile).astype(np.float64)[:, None] * inv[None, :]
    dup = lambda a: np.concatenate([a, a], axis=-1)
    sign = np.concatenate([-np.ones((half,)), np.ones((half,))])
    tile_rot = np.stack([dup(np.cos(base)), dup(np.sin(base))], axis=1)
    tile_rot = np.broadcast_to(tile_rot[:, :, None, :], (n_tiles, 2, SUBLANES, HEAD_DIM))
    cos_o, sin_o = dup(np.cos(off)), dup(np.sin(off))
    row_rot = np.stack([cos_o, sin_o, sign * cos_o, sign * sin_o])
    return _f32(tile_rot), _f32(row_rot)


def _lanes(per_head):
    return np.repeat(per_head, HEAD_DIM, axis=1)


def _decay_tables(rows, segment, causal_chunk):
    lg = _log_gamma()
    idx = np.arange(rows)
    loc = (idx % segment).astype(np.float64)
    same_seg = (idx[:, None] // segment) == (idx[None, :] // segment)
    visible = same_seg & ((idx[None, :] // causal_chunk) <= (idx[:, None] // causal_chunk))
    dist = np.abs(idx[:, None] - idx[None, :]).astype(np.float64)
    dmat = np.where(visible[None], np.exp(lg[:, None, None] * dist[None]), 0.0)
    d_in = _lanes(np.exp(lg[None, :] * (loc[:, None] + 1.0)))
    d_out = _lanes(np.exp(lg[None, :] * (segment - 1.0 - loc[:, None])))
    s_dec = np.broadcast_to(np.exp(lg * segment)[:, None], (RET_HEADS, HEAD_DIM))
    return _f32(dmat), _f32(d_in), _f32(d_out), _f32(s_dec)


def kernel(x_prompt, x_sample, state_ret, cache_conv, meta, g_pre_mix, w_in, gn_g, gn_b, dw_w, dw_b, cln_g, cln_b,
           w_out, g_post_mix, g_pre_mlp, w_mlp_in, w_mlp_out, g_post_mlp):
    batch, seq, _ = x_prompt.shape
    dec_batch, dec_seq, _ = x_sample.shape
    assert batch == 1 and dec_seq == STREAM_ROWS and meta.shape[0] == N_META == STREAM_ROWS
    assert seq % PROMPT_TILE == 0 and seq % MLP_TILE == 0 and dec_batch % SAMPLE_GROUP_STREAMS == 0
    assert (dec_batch * dec_seq) % MLP_TILE == 0 and state_ret.shape[0] == 1

    l = 0
    post = (gn_g[l][None], gn_b[l][None], dw_w[l], dw_b[l][None], cln_g[l][None], cln_b[l][None])
    mlp_w = (g_pre_mlp[l][None], _pack_weight(w_mlp_in[l]), _pack_weight(w_mlp_out[l]), g_post_mlp[l][None])

    group_rows = (SAMPLE_GROUP_STREAMS + 1) * STREAM_ROWS
    pos_stream = N_META + PAST_LEN + np.arange(dec_seq)
    pos_group = np.concatenate([np.tile(pos_stream, SAMPLE_GROUP_STREAMS), np.arange(N_META)])
    cos_s, sin_s = _rope_tables(pos_group)
    dec_s = _decay_tables(group_rows, STREAM_ROWS, STREAM_ROWS)
    xs1, s_s, c_s, s_meta, c_meta, w_in_packed, w_out_packed = _sample_mixer(
        x_sample.reshape(dec_batch * dec_seq, D_MODEL), meta, cos_s, sin_s, g_pre_mix[l][None], w_in[l], *dec_s,
        *post, w_out[l], g_post_mix[l][None], state_ret[l], jnp.swapaxes(cache_conv[l], 0, 1))

    rot_p = _prompt_rope_tables(seq // RET_BLOCK, RET_BLOCK)
    dec_p = _decay_tables(RET_BLOCK, RET_BLOCK, CHUNK)
    xp1, s_p, c_p = _prompt_mixer(x_prompt[0], *rot_p, g_pre_mix[l][None], w_in_packed, *dec_p, *post, w_out_packed,
                                  g_post_mix[l][None], s_meta, c_meta)
    yp, ys = _mlp(xp1, xs1, *mlp_w)

    return (yp[None], ys.reshape(dec_batch, dec_seq, D_MODEL), s_p[None, None], c_p[None, None],
            s_s[None], jnp.swapaxes(c_s, 0, 1)[None])
```

```python
import functools

import jax
import jax.numpy as jnp
import numpy as np
from jax import lax
from jax.experimental import pallas as pl
from jax.experimental.pallas import tpu as pltpu

D_MODEL = 1024
CHUNK = 64
N_META = 16
PAST_LEN = 4096
RET_WIDTH = 512
RET_HEADS = 4
HEAD_DIM = 128
CONV_WIDTH = 512
CONV_K = 31
D_FF = 4096
N_IN = 4 * RET_WIDTH + 2 * CONV_WIDTH
EPS = 1e-6
ROPE_BASE = 10000.0

LANES = 128
SUBLANES = 8
HIST = 32
HIST_OFF = HIST - (CONV_K - 1)
RET_BLOCK = 256
PROMPT_TILE = 1024
STREAM_ROWS = 16
SAMPLE_GROUP_STREAMS = 8
MLP_TILE = 512
CONV_ROW_BLOCK = 32
CONV_LANE_BLOCKS = CONV_WIDTH // LANES
VMEM_LIMIT_BYTES = 56 * 1024 * 1024

_F32 = jnp.float32
_BF16 = jnp.bfloat16


def _rms(x, g):
    return x * lax.rsqrt(jnp.mean(x * x, axis=-1, keepdims=True) + EPS) * g


def _silu(x):
    return x * jax.nn.sigmoid(x)


def _dot(a, b):
    return jnp.dot(a, b, preferred_element_type=_F32)


def _dot_nt(a, b):
    return lax.dot_general(a, b, (((1,), (1,)), ((), ())), preferred_element_type=_F32)


def _dot_tn(a, b):
    return lax.dot_general(a, b, (((0,), (0,)), ((), ())), preferred_element_type=_F32)


def _weight(packed):
    return pltpu.bitcast(packed, _BF16)


def _head(x, h):
    return x[:, h * HEAD_DIM:(h + 1) * HEAD_DIM]


def _rope(xh, cos_f, sin_f):
    return xh * cos_f + pltpu.roll(xh, HEAD_DIM // 2, 1) * sin_f


def _split_heads(q, k, v, cos_f, sin_f):
    qs = [_rope(_head(q, hh), cos_f, sin_f) for hh in range(RET_HEADS)]
    ks = [_rope(_head(k, hh), cos_f, sin_f) * (HEAD_DIM ** -0.5) for hh in range(RET_HEADS)]
    vs = [_head(v, hh) for hh in range(RET_HEADS)]
    return qs, ks, vs


def _project(x, g_pre, w_in, cos_f, sin_f):
    h = _rms(x, g_pre).astype(_BF16)
    proj = _dot(h, w_in)
    q = proj[:, 0:RET_WIDTH]
    k = proj[:, RET_WIDTH:2 * RET_WIDTH]
    v = proj[:, 2 * RET_WIDTH:3 * RET_WIDTH]
    gate = proj[:, 3 * RET_WIDTH:4 * RET_WIDTH]
    ga = proj[:, 4 * RET_WIDTH:4 * RET_WIDTH + CONV_WIDTH]
    gb = proj[:, 4 * RET_WIDTH + CONV_WIDTH:N_IN]
    qs, ks, vs = _split_heads(q, k, v, cos_f, sin_f)
    return qs, ks, vs, gate, ga * jax.nn.sigmoid(gb)


def _group_norm_gate(ret_h, gate_h, g, b):
    mu = jnp.mean(ret_h, axis=-1, keepdims=True)
    d = ret_h - mu
    var = jnp.mean(d * d, axis=-1, keepdims=True)
    rn = d * lax.rsqrt(var + EPS) * g + b
    return rn * _silu(gate_h)


def _layer_norm(x, g, b):
    mu = jnp.mean(x, axis=-1, keepdims=True)
    d = x - mu
    var = jnp.mean(d * d, axis=-1, keepdims=True)
    return d * lax.rsqrt(var + EPS) * g + b


def _causal_conv_lanes(ext_ref, dww_ref, dwb_ref, y_ref, n_rows, c):
    rb = min(CONV_ROW_BLOCK, n_rows)
    lanes = slice(c * LANES, (c + 1) * LANES)
    taps = [dww_ref[k:k + 1, lanes] for k in range(CONV_K)]
    bias = dwb_ref[:, lanes]
    for r0 in range(0, n_rows, rb):
        acc = ext_ref[c, r0 + HIST_OFF:r0 + HIST_OFF + rb, :] * taps[0]
        for k in range(1, CONV_K):
            acc = acc + ext_ref[c, r0 + HIST_OFF + k:r0 + HIST_OFF + k + rb, :] * taps[k]
        y_ref[r0:r0 + rb, lanes] = acc + bias


def _causal_conv(ext_ref, dww_ref, dwb_ref, y_ref, n_rows):
    for c in range(CONV_LANE_BLOCKS):
        _causal_conv_lanes(ext_ref, dww_ref, dwb_ref, y_ref, n_rows, c)


def _ext_store(ext_ref, r0, rows_value):
    n = rows_value.shape[0]
    for c in range(CONV_LANE_BLOCKS):
        ext_ref[c, r0:r0 + n, :] = rows_value[:, c * LANES:(c + 1) * LANES]


def _ext_load(ext_ref, r0, n):
    return jnp.concatenate([ext_ref[c, r0:r0 + n, :] for c in range(CONV_LANE_BLOCKS)], axis=-1)


def _mix_out(x, ret_out, conv_out, wout_ref, g_post):
    half = wout_ref.shape[0] // 2
    mix = (_dot(conv_out.astype(_BF16), _weight(wout_ref[half:, :]))
           + _dot(ret_out.astype(_BF16), _weight(wout_ref[0:half, :])))
    return x + _rms(mix, g_post)


def _prompt_mixer_kernel(x_ref, tile_rot_ref, row_rot_ref, gpre_ref, win_ref, dmat_ref, din_ref, dout_ref,
                         sdec_ref, gng_ref, gnb_ref, dww_ref, dwb_ref, clng_ref, clnb_ref, wout_ref, gpost_ref,
                         s0_ref, c0_ref, w1_ref, w2_ref,
                         x1_ref, sfin_ref, cfin_ref, w1_packed_ref, w2_packed_ref,
                         s_scr, ext_scr, y_scr):
    i = pl.program_id(0)
    rows = x_ref.shape[0]

    @pl.when(i == 0)
    def _():
        s_scr[...] = s0_ref[...]
        _ext_store(ext_scr, 0, jnp.zeros((HIST_OFF, CONV_WIDTH), _F32))
        _ext_store(ext_scr, HIST_OFF, c0_ref[...])

    w1_packed_ref[...] = pltpu.bitcast(w1_ref[...].astype(_BF16), jnp.uint32)
    w2_packed_ref[...] = pltpu.bitcast(w2_ref[...].astype(_BF16), jnp.uint32)

    x = x_ref[...]
    h = _rms(x, gpre_ref[...]).astype(_BF16)

    glu = _dot(h, _weight(win_ref[:, 4 * RET_WIDTH:N_IN]))
    u = glu[:, 0:CONV_WIDTH] * jax.nn.sigmoid(glu[:, CONV_WIDTH:2 * CONV_WIDTH])
    _ext_store(ext_scr, HIST, u)
    blocks = []
    for c in range(CONV_LANE_BLOCKS):
        blocks.append(_dot(h, _weight(win_ref[:, c * RET_WIDTH:(c + 1) * RET_WIDTH])))
        _causal_conv_lanes(ext_scr, dww_ref, dwb_ref, y_scr, rows, c)
    q, k, v, gate = blocks
    ext_scr[:, 0:HIST, :] = ext_scr[:, rows:rows + HIST, :]
    conv_out = _silu(_layer_norm(y_scr[...], clng_ref[...], clnb_ref[...]))

    ret_rows = []
    for b in range(rows // RET_BLOCK):
        rs = slice(b * RET_BLOCK, (b + 1) * RET_BLOCK)
        cos_a, sin_a = tile_rot_ref[b, 0, 0:1, :], tile_rot_ref[b, 1, 0:1, :]
        cos_f = cos_a * row_rot_ref[0] - sin_a * row_rot_ref[1]
        sin_f = sin_a * row_rot_ref[2] + cos_a * row_rot_ref[3]
        qs, ks, vs = _split_heads(q[rs], k[rs], v[rs], cos_f, sin_f)
        ret_parts = []
        for hh in range(RET_HEADS):
            hl = slice(hh * HEAD_DIM, (hh + 1) * HEAD_DIM)
            qb = qs[hh].astype(_BF16)
            vb = vs[hh].astype(_BF16)
            p = (_dot_nt(qb, ks[hh].astype(_BF16)) * dmat_ref[hh]).astype(_BF16)
            intra = _dot(p, vb)
            s_old = s_scr[hh]
            inter = _dot(qb, s_old.astype(_BF16)) * din_ref[:, hl]
            kd = (ks[hh] * dout_ref[:, hl]).astype(_BF16)
            s_scr[hh] = s_old * sdec_ref[hh:hh + 1, :] + _dot_tn(kd, vb)
            ret_parts.append(_group_norm_gate(intra + inter, gate[rs, hl], gng_ref[:, hl], gnb_ref[:, hl]))
        ret_rows.append(jnp.concatenate(ret_parts, axis=-1))
    ret_out = jnp.concatenate(ret_rows, axis=0)

    x1_ref[...] = _mix_out(x, ret_out, conv_out, wout_ref, gpost_ref[...])

    @pl.when(i == pl.num_programs(0) - 1)
    def _():
        sfin_ref[...] = s_scr[...]
        cfin_ref[...] = _ext_load(ext_scr, HIST_OFF, CONV_K - 1)


def _const_spec(shape):
    zeros = (0,) * len(shape)
    return pl.BlockSpec(shape, lambda i, z=zeros: z)


def _resident_spec(shape):
    zeros = (0,) * len(shape)
    return pl.BlockSpec(shape, lambda i, z=zeros: z, pipeline_mode=pl.Buffered(1))


def _prompt_mixer(x, tile_rot, row_rot, g_pre, w_in, dmat, d_in, d_out, s_dec, gn_g, gn_b, dw_w, dw_b, cln_g, cln_b,
                  w_out, g_post, s0, c0, w_mlp_in, w_mlp_out):
    n = x.shape[0]
    t = PROMPT_TILE
    steps = n // t
    consts = (row_rot, g_pre, w_in, dmat, d_in, d_out, s_dec, gn_g, gn_b, dw_w, dw_b, cln_g, cln_b, w_out,
              g_post, s0, c0)
    big = (w_in, w_out, dmat)
    side = (w_mlp_in, w_mlp_out)
    slices = [w.shape[0] // steps for w in side]
    assert all(w.shape[0] % steps == 0 and r % (2 * SUBLANES) == 0 for w, r in zip(side, slices))
    in_specs = ([pl.BlockSpec((t, D_MODEL), lambda i: (i, 0)),
                 pl.BlockSpec((t // RET_BLOCK,) + tile_rot.shape[1:], lambda i: (i, 0, 0, 0))]
                + [_resident_spec(c.shape) if any(c is b for b in big) else _const_spec(c.shape) for c in consts]
                + [pl.BlockSpec((r, w.shape[1]), lambda i: (i, 0)) for w, r in zip(side, slices)])
    out_shape = (jax.ShapeDtypeStruct((n, D_MODEL), _F32),
                 jax.ShapeDtypeStruct((RET_HEADS, HEAD_DIM, HEAD_DIM), _F32),
                 jax.ShapeDtypeStruct((CONV_K - 1, CONV_WIDTH), _F32),
                 jax.ShapeDtypeStruct((w_mlp_in.shape[0] // 2, w_mlp_in.shape[1]), jnp.uint32),
                 jax.ShapeDtypeStruct((w_mlp_out.shape[0] // 2, w_mlp_out.shape[1]), jnp.uint32))
    out_specs = (pl.BlockSpec((t, D_MODEL), lambda i: (i, 0)),
                 _const_spec((RET_HEADS, HEAD_DIM, HEAD_DIM)),
                 _const_spec((CONV_K - 1, CONV_WIDTH)),
                 pl.BlockSpec((slices[0] // 2, w_mlp_in.shape[1]), lambda i: (i, 0)),
                 pl.BlockSpec((slices[1] // 2, w_mlp_out.shape[1]), lambda i: (i, 0)))
    return pl.pallas_call(
        _prompt_mixer_kernel,
        grid=(steps,),
        in_specs=in_specs,
        out_specs=out_specs,
        out_shape=out_shape,
        scratch_shapes=[pltpu.VMEM((RET_HEADS, HEAD_DIM, HEAD_DIM), _F32),
                        pltpu.VMEM((CONV_LANE_BLOCKS, HIST + t, LANES), _F32),
                        pltpu.VMEM((t, CONV_WIDTH), _F32)],
        compiler_params=pltpu.CompilerParams(dimension_semantics=("arbitrary",),
                                             vmem_limit_bytes=VMEM_LIMIT_BYTES),
        name="prompt_mixer",
    )(x, tile_rot, *consts, *side)


def _sample_mixer_kernel(x_ref, meta_ref, cos_ref, sin_ref, gpre_ref, win_ref, dmat_ref, din_ref, dout_ref, sdec_ref,
                         gng_ref, gnb_ref, dww_ref, dwb_ref, clng_ref, clnb_ref, wout_ref, gpost_ref,
                         state_ref, cache_ref,
                         x1_ref, snew_ref, cnew_ref, smeta_ref, cmeta_ref, win_packed_ref, wout_packed_ref,
                         ext_scr, y_scr, extt_scr, uslab_scr, yslab_scr):
    n_streams = state_ref.shape[0]
    stream_rows = x_ref.shape[0]

    @pl.when(pl.program_id(0) == 0)
    def _():
        win_packed_ref[...] = pltpu.bitcast(win_ref[...].astype(_BF16), jnp.uint32)
        wout_packed_ref[...] = pltpu.bitcast(wout_ref[...].astype(_BF16), jnp.uint32)

    wout_ref = wout_packed_ref
    x = jnp.concatenate([x_ref[...], meta_ref[...]], axis=0)
    qs, ks, vs, gate, u = _project(x, gpre_ref[...], _weight(win_packed_ref[...]), cos_ref[...], sin_ref[...])

    ret_parts = []
    for h in range(RET_HEADS):
        hl = slice(h * HEAD_DIM, (h + 1) * HEAD_DIM)
        qb = qs[h].astype(_BF16)
        vb = vs[h].astype(_BF16)
        p = (_dot_nt(qb, ks[h].astype(_BF16)) * dmat_ref[h]).astype(_BF16)
        intra = _dot(p, vb)
        kd = (ks[h] * dout_ref[:, hl]).astype(_BF16)
        inter_parts = []
        for s in range(n_streams):
            rs = slice(s * STREAM_ROWS, (s + 1) * STREAM_ROWS)
            s_old = state_ref[s, h]
            inter_parts.append(_dot(qb[rs], s_old.astype(_BF16)))
            snew_ref[s, h] = s_old * sdec_ref[h:h + 1, :] + _dot_tn(kd[rs], vb[rs])
        ms = slice(stream_rows, stream_rows + STREAM_ROWS)
        inter_parts.append(jnp.zeros((STREAM_ROWS, HEAD_DIM), _F32))
        smeta_ref[h] = _dot_tn(kd[ms], vb[ms])
        inter = jnp.concatenate(inter_parts, axis=0) * din_ref[:, hl]
        ret_parts.append(_group_norm_gate(intra + inter, _head(gate, h), gng_ref[:, hl], gnb_ref[:, hl]))
    ret_out = jnp.concatenate(ret_parts, axis=-1)

    hist = CONV_K - 1
    for c in range(CONV_LANE_BLOCKS):
        uslab_scr[c] = u[0:stream_rows, c * LANES:(c + 1) * LANES]
    extt_scr[0:hist] = cache_ref[...]
    for t in range(STREAM_ROWS):
        extt_scr[hist + t] = jnp.concatenate(
            [uslab_scr[c, pl.ds(t, n_streams, stride=STREAM_ROWS), :] for c in range(CONV_LANE_BLOCKS)], axis=-1)
    for t in range(STREAM_ROWS):
        acc = extt_scr[t] * dww_ref[0:1, :]
        for k in range(1, CONV_K):
            acc = acc + extt_scr[t + k] * dww_ref[k:k + 1, :]
        y_t = acc + dwb_ref[...]
        for c in range(CONV_LANE_BLOCKS):
            yslab_scr[c, pl.ds(t, n_streams, stride=STREAM_ROWS), :] = y_t[:, c * LANES:(c + 1) * LANES]
    cnew_ref[...] = extt_scr[STREAM_ROWS:STREAM_ROWS + hist]

    _ext_store(ext_scr, 0, jnp.zeros((HIST, CONV_WIDTH), _F32))
    _ext_store(ext_scr, HIST, u[stream_rows:stream_rows + STREAM_ROWS])
    _causal_conv(ext_scr, dww_ref, dwb_ref, y_scr, STREAM_ROWS)
    cmeta_ref[...] = _ext_load(ext_scr, HIST + STREAM_ROWS - hist, hist)

    y_streams = jnp.concatenate([yslab_scr[c] for c in range(CONV_LANE_BLOCKS)], axis=-1)
    y_all = jnp.concatenate([y_streams, y_scr[...]], axis=0)
    conv_out = _silu(_layer_norm(y_all, clng_ref[...], clnb_ref[...]))

    x1_ref[...] = _mix_out(x, ret_out, conv_out, wout_ref, gpost_ref[...])[0:stream_rows]


def _sample_mixer(x_rows, meta, cos_f, sin_f, g_pre, w_in, dmat, d_in, d_out, s_dec, gn_g, gn_b, dw_w, dw_b, cln_g,
                  cln_b, w_out, g_post, state, cache):
    n = x_rows.shape[0]
    n_streams = state.shape[0]
    group = SAMPLE_GROUP_STREAMS
    rows = group * STREAM_ROWS
    consts = (meta, cos_f, sin_f, g_pre, w_in, dmat, d_in, d_out, s_dec, gn_g, gn_b, dw_w, dw_b, cln_g, cln_b, w_out,
              g_post)
    big = (w_in, w_out)
    assert group == SUBLANES
    cache_blk = pl.BlockSpec((CONV_K - 1, group, CONV_WIDTH), lambda i: (0, i, 0))
    in_specs = ([pl.BlockSpec((rows, D_MODEL), lambda i: (i, 0))]
                + [_resident_spec(c.shape) if any(c is b for b in big) else _const_spec(c.shape) for c in consts]
                + [pl.BlockSpec((group, RET_HEADS, HEAD_DIM, HEAD_DIM), lambda i: (i, 0, 0, 0)), cache_blk])
    packed = [(w.shape[0] // 2, w.shape[1]) for w in (w_in, w_out)]
    out_shape = (jax.ShapeDtypeStruct((n, D_MODEL), _F32),
                 jax.ShapeDtypeStruct((n_streams, RET_HEADS, HEAD_DIM, HEAD_DIM), _F32),
                 jax.ShapeDtypeStruct((CONV_K - 1, n_streams, CONV_WIDTH), _F32),
                 jax.ShapeDtypeStruct((RET_HEADS, HEAD_DIM, HEAD_DIM), _F32),
                 jax.ShapeDtypeStruct((CONV_K - 1, CONV_WIDTH), _F32),
                 jax.ShapeDtypeStruct(packed[0], jnp.uint32),
                 jax.ShapeDtypeStruct(packed[1], jnp.uint32))
    out_specs = (pl.BlockSpec((rows, D_MODEL), lambda i: (i, 0)),
                 pl.BlockSpec((group, RET_HEADS, HEAD_DIM, HEAD_DIM), lambda i: (i, 0, 0, 0)),
                 cache_blk,
                 _const_spec((RET_HEADS, HEAD_DIM, HEAD_DIM)),
                 _const_spec((CONV_K - 1, CONV_WIDTH)),
                 _const_spec(packed[0]),
                 _const_spec(packed[1]))
    return pl.pallas_call(
        _sample_mixer_kernel,
        grid=(n_streams // group,),
        in_specs=in_specs,
        out_specs=out_specs,
        out_shape=out_shape,
        scratch_shapes=[pltpu.VMEM((CONV_LANE_BLOCKS, HIST + STREAM_ROWS, LANES), _F32),
                        pltpu.VMEM((STREAM_ROWS, CONV_WIDTH), _F32),
                        pltpu.VMEM((CONV_K - 1 + STREAM_ROWS, group, CONV_WIDTH), _F32),
                        pltpu.VMEM((CONV_LANE_BLOCKS, rows, LANES), _F32),
                        pltpu.VMEM((CONV_LANE_BLOCKS, rows, LANES), _F32)],
        compiler_params=pltpu.CompilerParams(dimension_semantics=("arbitrary",),
                                             vmem_limit_bytes=VMEM_LIMIT_BYTES),
        name="sample_mixer",
    )(x_rows, *consts, state, cache)


def _mlp_rows(x, g_pre, w1_ref, w2_ref, g_post):
    h = _rms(x, g_pre).astype(_BF16)
    a = jnp.maximum(_dot(h, _weight(w1_ref[...])), 0.0)
    f = _dot((a * a).astype(_BF16), _weight(w2_ref[...]))
    return x + _rms(f, g_post)


def _mlp_kernel(xa_ref, xb_ref, gpre_ref, w1_ref, w2_ref, gpost_ref, ya_ref, yb_ref, *, a_tiles):
    i = pl.program_id(0)

    @pl.when(i < a_tiles)
    def _():
        ya_ref[...] = _mlp_rows(xa_ref[...], gpre_ref[...], w1_ref, w2_ref, gpost_ref[...])

    @pl.when(i >= a_tiles)
    def _():
        yb_ref[...] = _mlp_rows(xb_ref[...], gpre_ref[...], w1_ref, w2_ref, gpost_ref[...])


def _mlp(xa, xb, g_pre, w1, w2, g_post):
    t = MLP_TILE
    a_tiles, b_tiles = xa.shape[0] // t, xb.shape[0] // t
    a_blk = lambda i: (jnp.minimum(i, a_tiles - 1), 0)
    b_blk = lambda i: (jnp.maximum(i - a_tiles, 0), 0)
    return pl.pallas_call(
        functools.partial(_mlp_kernel, a_tiles=a_tiles),
        grid=(a_tiles + b_tiles,),
        in_specs=[pl.BlockSpec((t, D_MODEL), a_blk),
                  pl.BlockSpec((t, D_MODEL), b_blk),
                  _const_spec(g_pre.shape),
                  _resident_spec(w1.shape),
                  _resident_spec(w2.shape),
                  _const_spec(g_post.shape)],
        out_specs=(pl.BlockSpec((t, D_MODEL), a_blk), pl.BlockSpec((t, D_MODEL), b_blk)),
        out_shape=(jax.ShapeDtypeStruct(xa.shape, _F32), jax.ShapeDtypeStruct(xb.shape, _F32)),
        compiler_params=pltpu.CompilerParams(dimension_semantics=("arbitrary",),
                                             vmem_limit_bytes=VMEM_LIMIT_BYTES),
        name="mlp",
    )(xa, xb, g_pre, w1, w2, g_post)


def _f32(a):
    return np.ascontiguousarray(a, dtype=np.float32)


def _log_gamma():
    return np.log1p(-np.exp2(-5.0 - np.arange(RET_HEADS, dtype=np.float64)))


def _rope_freqs():
    half = HEAD_DIM // 2
    return ROPE_BASE ** (-np.arange(half, dtype=np.float64) / half)


def _rope_tables(pos):
    ang = pos.astype(np.float64)[:, None] * _rope_freqs()[None, :]
    cos = np.cos(ang)
    sin = np.sin(ang)
    return _f32(np.concatenate([cos, cos], axis=-1)), _f32(np.concatenate([-sin, sin], axis=-1))


def _prompt_rope_tables(n_tiles, tile):
    half = HEAD_DIM // 2
    inv = _rope_freqs()
    base = (N_META + tile * np.arange(n_tiles)).astype(np.float64)[:, None] * inv[None, :]
    off = np.arange(tile).astype(np.float64)[:, None] * inv[None, :]
    dup = lambda a: np.concatenate([a, a], axis=-1)
    sign = np.concatenate([-np.ones((half,)), np.ones((half,))])
    tile_rot = np.stack([dup(np.cos(base)), dup(np.sin(base))], axis=1)
    tile_rot = np.broadcast_to(tile_rot[:, :, None, :], (n_tiles, 2, SUBLANES, HEAD_DIM))
    cos_o, sin_o = dup(np.cos(off)), dup(np.sin(off))
    row_rot = np.stack([cos_o, sin_o, sign * cos_o, sign * sin_o])
    return _f32(tile_rot), _f32(row_rot)


def _lanes(per_head):
    return np.repeat(per_head, HEAD_DIM, axis=1)


def _decay_tables(rows, segment, causal_chunk):
    lg = _log_gamma()
    idx = np.arange(rows)
    loc = (idx % segment).astype(np.float64)
    same_seg = (idx[:, None] // segment) == (idx[None, :] // segment)
    visible = same_seg & ((idx[None, :] // causal_chunk) <= (idx[:, None] // causal_chunk))
    dist = np.abs(idx[:, None] - idx[None, :]).astype(np.float64)
    dmat = np.where(visible[None], np.exp(lg[:, None, None] * dist[None]), 0.0)
    d_in = _lanes(np.exp(lg[None, :] * (loc[:, None] + 1.0)))
    d_out = _lanes(np.exp(lg[None, :] * (segment - 1.0 - loc[:, None])))
    s_dec = np.broadcast_to(np.exp(lg * segment)[:, None], (RET_HEADS, HEAD_DIM))
    return _f32(dmat), _f32(d_in), _f32(d_out), _f32(s_dec)


def kernel(x_prompt, x_sample, state_ret, cache_conv, meta, g_pre_mix, w_in, gn_g, gn_b, dw_w, dw_b, cln_g, cln_b,
           w_out, g_post_mix, g_pre_mlp, w_mlp_in, w_mlp_out, g_post_mlp):
    batch, seq, _ = x_prompt.shape
    dec_batch, dec_seq, _ = x_sample.shape
    assert batch == 1 and dec_seq == STREAM_ROWS and meta.shape[0] == N_META == STREAM_ROWS
    assert seq % PROMPT_TILE == 0 and seq % MLP_TILE == 0 and dec_batch % SAMPLE_GROUP_STREAMS == 0
    assert (dec_batch * dec_seq) % MLP_TILE == 0 and state_ret.shape[0] == 1

    l = 0
    post = (gn_g[l][None], gn_b[l][None], dw_w[l], dw_b[l][None], cln_g[l][None], cln_b[l][None])

    group_rows = (SAMPLE_GROUP_STREAMS + 1) * STREAM_ROWS
    pos_stream = N_META + PAST_LEN + np.arange(dec_seq)
    pos_group = np.concatenate([np.tile(pos_stream, SAMPLE_GROUP_STREAMS), np.arange(N_META)])
    cos_s, sin_s = _rope_tables(pos_group)
    dec_s = _decay_tables(group_rows, STREAM_ROWS, STREAM_ROWS)
    xs1, s_s, c_s, s_meta, c_meta, w_in_packed, w_out_packed = _sample_mixer(
        x_sample.reshape(dec_batch * dec_seq, D_MODEL), meta, cos_s, sin_s, g_pre_mix[l][None], w_in[l], *dec_s,
        *post, w_out[l], g_post_mix[l][None], state_ret[l], jnp.swapaxes(cache_conv[l], 0, 1))

    rot_p = _prompt_rope_tables(seq // RET_BLOCK, RET_BLOCK)
    dec_p = _decay_tables(RET_BLOCK, RET_BLOCK, CHUNK)
    xp1, s_p, c_p, w1_packed, w2_packed = _prompt_mixer(
        x_prompt[0], *rot_p, g_pre_mix[l][None], w_in_packed, *dec_p, *post, w_out_packed, g_post_mix[l][None],
        s_meta, c_meta, w_mlp_in[l], w_mlp_out[l])
    yp, ys = _mlp(xp1, xs1, g_pre_mlp[l][None], w1_packed, w2_packed, g_post_mlp[l][None])

    return (yp[None], ys.reshape(dec_batch, dec_seq, D_MODEL), s_p[None, None], c_p[None, None],
            s_s[None], jnp.swapaxes(c_s, 0, 1)[None])
```

```python
import functools

import jax
import jax.numpy as jnp
import numpy as np
from jax import lax
from jax.experimental import pallas as pl
from jax.experimental.pallas import tpu as pltpu

D_MODEL = 1024
CHUNK = 64
N_META = 16
PAST_LEN = 4096
RET_WIDTH = 512
RET_HEADS = 4
HEAD_DIM = 128
CONV_WIDTH = 512
CONV_K = 31
D_FF = 4096
N_IN = 4 * RET_WIDTH + 2 * CONV_WIDTH
EPS = 1e-6
ROPE_BASE = 10000.0

LANES = 128
SUBLANES = 8
HIST = 32
HIST_OFF = HIST - (CONV_K - 1)
RET_BLOCK = 256
PROMPT_TILE = 1024
STREAM_ROWS = 16
SAMPLE_GROUP_STREAMS = 8
MLP_TILE = 1024
MLP_TILE_SMALL = 512
MLP_FF_CHUNK = 1024
CONV_ROW_BLOCK = 32
CONV_LANE_BLOCKS = CONV_WIDTH // LANES
VMEM_LIMIT_BYTES = 56 * 1024 * 1024

_F32 = jnp.float32
_BF16 = jnp.bfloat16


def _rms(x, g):
    return x * lax.rsqrt(jnp.mean(x * x, axis=-1, keepdims=True) + EPS) * g


def _silu(x):
    return x * jax.nn.sigmoid(x)


def _dot(a, b):
    return jnp.dot(a, b, preferred_element_type=_F32)


def _dot_nt(a, b):
    return lax.dot_general(a, b, (((1,), (1,)), ((), ())), preferred_element_type=_F32)


def _dot_tn(a, b):
    return lax.dot_general(a, b, (((0,), (0,)), ((), ())), preferred_element_type=_F32)


def _weight(packed):
    return pltpu.bitcast(packed, _BF16)


def _head(x, h):
    return x[:, h * HEAD_DIM:(h + 1) * HEAD_DIM]


def _rope(xh, cos_f, sin_f):
    return xh * cos_f + pltpu.roll(xh, HEAD_DIM // 2, 1) * sin_f


def _split_heads(q, k, v, cos_f, sin_f):
    qs = [_rope(_head(q, hh), cos_f, sin_f) for hh in range(RET_HEADS)]
    ks = [_rope(_head(k, hh), cos_f, sin_f) * (HEAD_DIM ** -0.5) for hh in range(RET_HEADS)]
    vs = [_head(v, hh) for hh in range(RET_HEADS)]
    return qs, ks, vs


def _project(x, g_pre, w_in, cos_f, sin_f):
    h = _rms(x, g_pre).astype(_BF16)
    proj = _dot(h, w_in)
    q = proj[:, 0:RET_WIDTH]
    k = proj[:, RET_WIDTH:2 * RET_WIDTH]
    v = proj[:, 2 * RET_WIDTH:3 * RET_WIDTH]
    gate = proj[:, 3 * RET_WIDTH:4 * RET_WIDTH]
    ga = proj[:, 4 * RET_WIDTH:4 * RET_WIDTH + CONV_WIDTH]
    gb = proj[:, 4 * RET_WIDTH + CONV_WIDTH:N_IN]
    qs, ks, vs = _split_heads(q, k, v, cos_f, sin_f)
    return qs, ks, vs, gate, ga * jax.nn.sigmoid(gb)


def _group_norm_gate(ret_h, gate_h, g, b):
    mu = jnp.mean(ret_h, axis=-1, keepdims=True)
    d = ret_h - mu
    var = jnp.mean(d * d, axis=-1, keepdims=True)
    rn = d * lax.rsqrt(var + EPS) * g + b
    return rn * _silu(gate_h)


def _layer_norm(x, g, b):
    mu = jnp.mean(x, axis=-1, keepdims=True)
    d = x - mu
    var = jnp.mean(d * d, axis=-1, keepdims=True)
    return d * lax.rsqrt(var + EPS) * g + b


def _causal_conv_lanes(ext_ref, dww_ref, dwb_ref, y_ref, n_rows, c):
    rb = min(CONV_ROW_BLOCK, n_rows)
    lanes = slice(c * LANES, (c + 1) * LANES)
    taps = [dww_ref[k:k + 1, lanes] for k in range(CONV_K)]
    bias = dwb_ref[:, lanes]
    for r0 in range(0, n_rows, rb):
        acc = ext_ref[c, r0 + HIST_OFF:r0 + HIST_OFF + rb, :] * taps[0]
        for k in range(1, CONV_K):
            acc = acc + ext_ref[c, r0 + HIST_OFF + k:r0 + HIST_OFF + k + rb, :] * taps[k]
        y_ref[r0:r0 + rb, lanes] = acc + bias


def _causal_conv(ext_ref, dww_ref, dwb_ref, y_ref, n_rows):
    for c in range(CONV_LANE_BLOCKS):
        _causal_conv_lanes(ext_ref, dww_ref, dwb_ref, y_ref, n_rows, c)


def _ext_store(ext_ref, r0, rows_value):
    n = rows_value.shape[0]
    for c in range(CONV_LANE_BLOCKS):
        ext_ref[c, r0:r0 + n, :] = rows_value[:, c * LANES:(c + 1) * LANES]


def _ext_load(ext_ref, r0, n):
    return jnp.concatenate([ext_ref[c, r0:r0 + n, :] for c in range(CONV_LANE_BLOCKS)], axis=-1)


def _mix_out(x, ret_out, conv_out, wout_ref, g_post):
    half = wout_ref.shape[0] // 2
    mix = (_dot(conv_out.astype(_BF16), _weight(wout_ref[half:, :]))
           + _dot(ret_out.astype(_BF16), _weight(wout_ref[0:half, :])))
    return x + _rms(mix, g_post)


def _prompt_mixer_kernel(x_ref, tile_rot_ref, row_rot_ref, gpre_ref, win_ref, dmat_ref, din_ref, dout_ref,
                         sdec_ref, gng_ref, gnb_ref, dww_ref, dwb_ref, clng_ref, clnb_ref, wout_ref, gpost_ref,
                         s0_ref, c0_ref, w1_ref, w2_ref,
                         x1_ref, sfin_ref, cfin_ref, w1_packed_ref, w2_packed_ref,
                         s_scr, ext_scr, y_scr):
    i = pl.program_id(0)
    rows = x_ref.shape[0]

    @pl.when(i == 0)
    def _():
        s_scr[...] = s0_ref[...]
        _ext_store(ext_scr, 0, jnp.zeros((HIST_OFF, CONV_WIDTH), _F32))
        _ext_store(ext_scr, HIST_OFF, c0_ref[...])

    w1_packed_ref[...] = pltpu.bitcast(w1_ref[...].astype(_BF16), jnp.uint32)
    w2_packed_ref[...] = pltpu.bitcast(w2_ref[...].astype(_BF16), jnp.uint32)

    x = x_ref[...]
    h = _rms(x, gpre_ref[...]).astype(_BF16)

    glu = _dot(h, _weight(win_ref[:, 4 * RET_WIDTH:N_IN]))
    u = glu[:, 0:CONV_WIDTH] * jax.nn.sigmoid(glu[:, CONV_WIDTH:2 * CONV_WIDTH])
    _ext_store(ext_scr, HIST, u)
    blocks = []
    for c in range(CONV_LANE_BLOCKS):
        blocks.append(_dot(h, _weight(win_ref[:, c * RET_WIDTH:(c + 1) * RET_WIDTH])))
        _causal_conv_lanes(ext_scr, dww_ref, dwb_ref, y_scr, rows, c)
    q, k, v, gate = blocks
    ext_scr[:, 0:HIST, :] = ext_scr[:, rows:rows + HIST, :]
    conv_out = _silu(_layer_norm(y_scr[...], clng_ref[...], clnb_ref[...]))

    ret_rows = []
    for b in range(rows // RET_BLOCK):
        rs = slice(b * RET_BLOCK, (b + 1) * RET_BLOCK)
        cos_a, sin_a = tile_rot_ref[b, 0, 0:1, :], tile_rot_ref[b, 1, 0:1, :]
        cos_f = cos_a * row_rot_ref[0] - sin_a * row_rot_ref[1]
        sin_f = sin_a * row_rot_ref[2] + cos_a * row_rot_ref[3]
        qs, ks, vs = _split_heads(q[rs], k[rs], v[rs], cos_f, sin_f)
        ret_parts = []
        for hh in range(RET_HEADS):
            hl = slice(hh * HEAD_DIM, (hh + 1) * HEAD_DIM)
            qb = qs[hh].astype(_BF16)
            vb = vs[hh].astype(_BF16)
            p = (_dot_nt(qb, ks[hh].astype(_BF16)) * dmat_ref[hh]).astype(_BF16)
            intra = _dot(p, vb)
            s_old = s_scr[hh]
            inter = _dot(qb, s_old.astype(_BF16)) * din_ref[:, hl]
            kd = (ks[hh] * dout_ref[:, hl]).astype(_BF16)
            s_scr[hh] = s_old * sdec_ref[hh:hh + 1, :] + _dot_tn(kd, vb)
            ret_parts.append(_group_norm_gate(intra + inter, gate[rs, hl], gng_ref[:, hl], gnb_ref[:, hl]))
        ret_rows.append(jnp.concatenate(ret_parts, axis=-1))
    ret_out = jnp.concatenate(ret_rows, axis=0)

    x1_ref[...] = _mix_out(x, ret_out, conv_out, wout_ref, gpost_ref[...])

    @pl.when(i == pl.num_programs(0) - 1)
    def _():
        sfin_ref[...] = s_scr[...]
        cfin_ref[...] = _ext_load(ext_scr, HIST_OFF, CONV_K - 1)


def _const_spec(shape):
    zeros = (0,) * len(shape)
    return pl.BlockSpec(shape, lambda i, z=zeros: z)


def _resident_spec(shape):
    zeros = (0,) * len(shape)
    return pl.BlockSpec(shape, lambda i, z=zeros: z, pipeline_mode=pl.Buffered(1))


def _prompt_mixer(x, tile_rot, row_rot, g_pre, w_in, dmat, d_in, d_out, s_dec, gn_g, gn_b, dw_w, dw_b, cln_g, cln_b,
                  w_out, g_post, s0, c0, w_mlp_in, w_mlp_out):
    n = x.shape[0]
    t = PROMPT_TILE
    steps = n // t
    consts = (row_rot, g_pre, w_in, dmat, d_in, d_out, s_dec, gn_g, gn_b, dw_w, dw_b, cln_g, cln_b, w_out,
              g_post, s0, c0)
    big = (w_in, w_out, dmat)
    side = (w_mlp_in, w_mlp_out)
    slices = [w.shape[0] // steps for w in side]
    assert all(w.shape[0] % steps == 0 and r % (2 * SUBLANES) == 0 for w, r in zip(side, slices))
    in_specs = ([pl.BlockSpec((t, D_MODEL), lambda i: (i, 0)),
                 pl.BlockSpec((t // RET_BLOCK,) + tile_rot.shape[1:], lambda i: (i, 0, 0, 0))]
                + [_resident_spec(c.shape) if any(c is b for b in big) else _const_spec(c.shape) for c in consts]
                + [pl.BlockSpec((r, w.shape[1]), lambda i: (i, 0)) for w, r in zip(side, slices)])
    out_shape = (jax.ShapeDtypeStruct((n, D_MODEL), _F32),
                 jax.ShapeDtypeStruct((RET_HEADS, HEAD_DIM, HEAD_DIM), _F32),
                 jax.ShapeDtypeStruct((CONV_K - 1, CONV_WIDTH), _F32),
                 jax.ShapeDtypeStruct((w_mlp_in.shape[0] // 2, w_mlp_in.shape[1]), jnp.uint32),
                 jax.ShapeDtypeStruct((w_mlp_out.shape[0] // 2, w_mlp_out.shape[1]), jnp.uint32))
    out_specs = (pl.BlockSpec((t, D_MODEL), lambda i: (i, 0)),
                 _const_spec((RET_HEADS, HEAD_DIM, HEAD_DIM)),
                 _const_spec((CONV_K - 1, CONV_WIDTH)),
                 pl.BlockSpec((slices[0] // 2, w_mlp_in.shape[1]), lambda i: (i, 0)),
                 pl.BlockSpec((slices[1] // 2, w_mlp_out.shape[1]), lambda i: (i, 0)))
    return pl.pallas_call(
        _prompt_mixer_kernel,
        grid=(steps,),
        in_specs=in_specs,
        out_specs=out_specs,
        out_shape=out_shape,
        scratch_shapes=[pltpu.VMEM((RET_HEADS, HEAD_DIM, HEAD_DIM), _F32),
                        pltpu.VMEM((CONV_LANE_BLOCKS, HIST + t, LANES), _F32),
                        pltpu.VMEM((t, CONV_WIDTH), _F32)],
        compiler_params=pltpu.CompilerParams(dimension_semantics=("arbitrary",),
                                             vmem_limit_bytes=VMEM_LIMIT_BYTES),
        name="prompt_mixer",
    )(x, tile_rot, *consts, *side)


def _sample_mixer_kernel(x_ref, meta_ref, cos_ref, sin_ref, gpre_ref, win_ref, dmat_ref, din_ref, dout_ref, sdec_ref,
                         gng_ref, gnb_ref, dww_ref, dwb_ref, clng_ref, clnb_ref, wout_ref, gpost_ref,
                         state_ref, cache_ref,
                         x1_ref, snew_ref, cnew_ref, smeta_ref, cmeta_ref, win_packed_ref, wout_packed_ref,
                         ext_scr, y_scr, extt_scr, uslab_scr, yslab_scr):
    n_streams = state_ref.shape[0]
    stream_rows = x_ref.shape[0]

    @pl.when(pl.program_id(0) == 0)
    def _():
        win_packed_ref[...] = pltpu.bitcast(win_ref[...].astype(_BF16), jnp.uint32)
        wout_packed_ref[...] = pltpu.bitcast(wout_ref[...].astype(_BF16), jnp.uint32)

    wout_ref = wout_packed_ref
    x = jnp.concatenate([x_ref[...], meta_ref[...]], axis=0)
    qs, ks, vs, gate, u = _project(x, gpre_ref[...], _weight(win_packed_ref[...]), cos_ref[...], sin_ref[...])

    ret_parts = []
    for h in range(RET_HEADS):
        hl = slice(h * HEAD_DIM, (h + 1) * HEAD_DIM)
        qb = qs[h].astype(_BF16)
        vb = vs[h].astype(_BF16)
        p = (_dot_nt(qb, ks[h].astype(_BF16)) * dmat_ref[h]).astype(_BF16)
        intra = _dot(p, vb)
        kd = (ks[h] * dout_ref[:, hl]).astype(_BF16)
        inter_parts = []
        for s in range(n_streams):
            rs = slice(s * STREAM_ROWS, (s + 1) * STREAM_ROWS)
            s_old = state_ref[s, h]
            inter_parts.append(_dot(qb[rs], s_old.astype(_BF16)))
            snew_ref[s, h] = s_old * sdec_ref[h:h + 1, :] + _dot_tn(kd[rs], vb[rs])
        ms = slice(stream_rows, stream_rows + STREAM_ROWS)
        inter_parts.append(jnp.zeros((STREAM_ROWS, HEAD_DIM), _F32))
        smeta_ref[h] = _dot_tn(kd[ms], vb[ms])
        inter = jnp.concatenate(inter_parts, axis=0) * din_ref[:, hl]
        ret_parts.append(_group_norm_gate(intra + inter, _head(gate, h), gng_ref[:, hl], gnb_ref[:, hl]))
    ret_out = jnp.concatenate(ret_parts, axis=-1)

    hist = CONV_K - 1
    for c in range(CONV_LANE_BLOCKS):
        uslab_scr[c] = u[0:stream_rows, c * LANES:(c + 1) * LANES]
    extt_scr[0:hist] = cache_ref[...]
    for t in range(STREAM_ROWS):
        extt_scr[hist + t] = jnp.concatenate(
            [uslab_scr[c, pl.ds(t, n_streams, stride=STREAM_ROWS), :] for c in range(CONV_LANE_BLOCKS)], axis=-1)
    for t in range(STREAM_ROWS):
        acc = extt_scr[t] * dww_ref[0:1, :]
        for k in range(1, CONV_K):
            acc = acc + extt_scr[t + k] * dww_ref[k:k + 1, :]
        y_t = acc + dwb_ref[...]
        for c in range(CONV_LANE_BLOCKS):
            yslab_scr[c, pl.ds(t, n_streams, stride=STREAM_ROWS), :] = y_t[:, c * LANES:(c + 1) * LANES]
    cnew_ref[...] = extt_scr[STREAM_ROWS:STREAM_ROWS + hist]

    _ext_store(ext_scr, 0, jnp.zeros((HIST, CONV_WIDTH), _F32))
    _ext_store(ext_scr, HIST, u[stream_rows:stream_rows + STREAM_ROWS])
    _causal_conv(ext_scr, dww_ref, dwb_ref, y_scr, STREAM_ROWS)
    cmeta_ref[...] = _ext_load(ext_scr, HIST + STREAM_ROWS - hist, hist)

    y_streams = jnp.concatenate([yslab_scr[c] for c in range(CONV_LANE_BLOCKS)], axis=-1)
    y_all = jnp.concatenate([y_streams, y_scr[...]], axis=0)
    conv_out = _silu(_layer_norm(y_all, clng_ref[...], clnb_ref[...]))

    x1_ref[...] = _mix_out(x, ret_out, conv_out, wout_ref, gpost_ref[...])[0:stream_rows]


def _sample_mixer(x_rows, meta, cos_f, sin_f, g_pre, w_in, dmat, d_in, d_out, s_dec, gn_g, gn_b, dw_w, dw_b, cln_g,
                  cln_b, w_out, g_post, state, cache):
    n = x_rows.shape[0]
    n_streams = state.shape[0]
    group = SAMPLE_GROUP_STREAMS
    rows = group * STREAM_ROWS
    consts = (meta, cos_f, sin_f, g_pre, w_in, dmat, d_in, d_out, s_dec, gn_g, gn_b, dw_w, dw_b, cln_g, cln_b, w_out,
              g_post)
    big = (w_in, w_out)
    assert group == SUBLANES
    cache_blk = pl.BlockSpec((CONV_K - 1, group, CONV_WIDTH), lambda i: (0, i, 0))
    in_specs = ([pl.BlockSpec((rows, D_MODEL), lambda i: (i, 0))]
                + [_resident_spec(c.shape) if any(c is b for b in big) else _const_spec(c.shape) for c in consts]
                + [pl.BlockSpec((group, RET_HEADS, HEAD_DIM, HEAD_DIM), lambda i: (i, 0, 0, 0)), cache_blk])
    packed = [(w.shape[0] // 2, w.shape[1]) for w in (w_in, w_out)]
    out_shape = (jax.ShapeDtypeStruct((n, D_MODEL), _F32),
                 jax.ShapeDtypeStruct((n_streams, RET_HEADS, HEAD_DIM, HEAD_DIM), _F32),
                 jax.ShapeDtypeStruct((CONV_K - 1, n_streams, CONV_WIDTH), _F32),
                 jax.ShapeDtypeStruct((RET_HEADS, HEAD_DIM, HEAD_DIM), _F32),
                 jax.ShapeDtypeStruct((CONV_K - 1, CONV_WIDTH), _F32),
                 jax.ShapeDtypeStruct(packed[0], jnp.uint32),
                 jax.ShapeDtypeStruct(packed[1], jnp.uint32))
    out_specs = (pl.BlockSpec((rows, D_MODEL), lambda i: (i, 0)),
                 pl.BlockSpec((group, RET_HEADS, HEAD_DIM, HEAD_DIM), lambda i: (i, 0, 0, 0)),
                 cache_blk,
                 _const_spec((RET_HEADS, HEAD_DIM, HEAD_DIM)),
                 _const_spec((CONV_K - 1, CONV_WIDTH)),
                 _const_spec(packed[0]),
                 _const_spec(packed[1]))
    return pl.pallas_call(
        _sample_mixer_kernel,
        grid=(n_streams // group,),
        in_specs=in_specs,
        out_specs=out_specs,
        out_shape=out_shape,
        scratch_shapes=[pltpu.VMEM((CONV_LANE_BLOCKS, HIST + STREAM_ROWS, LANES), _F32),
                        pltpu.VMEM((STREAM_ROWS, CONV_WIDTH), _F32),
                        pltpu.VMEM((CONV_K - 1 + STREAM_ROWS, group, CONV_WIDTH), _F32),
                        pltpu.VMEM((CONV_LANE_BLOCKS, rows, LANES), _F32),
                        pltpu.VMEM((CONV_LANE_BLOCKS, rows, LANES), _F32)],
        compiler_params=pltpu.CompilerParams(dimension_semantics=("arbitrary",),
                                             vmem_limit_bytes=VMEM_LIMIT_BYTES),
        name="sample_mixer",
    )(x_rows, *consts, state, cache)


def _mlp_rows(x, g_pre, w1_ref, w2_ref, g_post):
    h = _rms(x, g_pre).astype(_BF16)
    f = None
    for c0 in range(0, D_FF, MLP_FF_CHUNK):
        a = jnp.maximum(_dot(h, _weight(w1_ref[:, c0:c0 + MLP_FF_CHUNK])), 0.0)
        f_c = _dot((a * a).astype(_BF16), _weight(w2_ref[c0 // 2:(c0 + MLP_FF_CHUNK) // 2, :]))
        f = f_c if f is None else f + f_c
    return x + _rms(f, g_post)


def _mlp_kernel(xa_ref, xb_ref, gpre_ref, w1_ref, w2_ref, gpost_ref, ya_ref, yb_ref, *, a_tiles):
    i = pl.program_id(0)

    @pl.when(i < a_tiles)
    def _():
        ya_ref[...] = _mlp_rows(xa_ref[...], gpre_ref[...], w1_ref, w2_ref, gpost_ref[...])

    @pl.when(i >= a_tiles)
    def _():
        yb_ref[...] = _mlp_rows(xb_ref[...], gpre_ref[...], w1_ref, w2_ref, gpost_ref[...])


def _mlp(xa, xb, g_pre, w1, w2, g_post):
    ta, tb = MLP_TILE, MLP_TILE_SMALL
    a_tiles, b_tiles = xa.shape[0] // ta, xb.shape[0] // tb
    a_blk = lambda i: (jnp.minimum(i, a_tiles - 1), 0)
    b_blk = lambda i: (jnp.maximum(i - a_tiles, 0), 0)
    return pl.pallas_call(
        functools.partial(_mlp_kernel, a_tiles=a_tiles),
        grid=(a_tiles + b_tiles,),
        in_specs=[pl.BlockSpec((ta, D_MODEL), a_blk),
                  pl.BlockSpec((tb, D_MODEL), b_blk),
                  _const_spec(g_pre.shape),
                  _resident_spec(w1.shape),
                  _resident_spec(w2.shape),
                  _const_spec(g_post.shape)],
        out_specs=(pl.BlockSpec((ta, D_MODEL), a_blk), pl.BlockSpec((tb, D_MODEL), b_blk)),
        out_shape=(jax.ShapeDtypeStruct(xa.shape, _F32), jax.ShapeDtypeStruct(xb.shape, _F32)),
        compiler_params=pltpu.CompilerParams(dimension_semantics=("arbitrary",),
                                             vmem_limit_bytes=VMEM_LIMIT_BYTES),
        name="mlp",
    )(xa, xb, g_pre, w1, w2, g_post)


def _f32(a):
    return np.ascontiguousarray(a, dtype=np.float32)


def _log_gamma():
    return np.log1p(-np.exp2(-5.0 - np.arange(RET_HEADS, dtype=np.float64)))


def _rope_freqs():
    half = HEAD_DIM // 2
    return ROPE_BASE ** (-np.arange(half, dtype=np.float64) / half)


def _rope_tables(pos):
    ang = pos.astype(np.float64)[:, None] * _rope_freqs()[None, :]
    cos = np.cos(ang)
    sin = np.sin(ang)
    return _f32(np.concatenate([cos, cos], axis=-1)), _f32(np.concatenate([-sin, sin], axis=-1))


def _prompt_rope_tables(n_tiles, tile):
    half = HEAD_DIM // 2
    inv = _rope_freqs()
    base = (N_META + tile * np.arange(n_tiles)).astype(np.float64)[:, None] * inv[None, :]
    off = np.arange(tile).astype(np.float64)[:, None] * inv[None, :]
    dup = lambda a: np.concatenate([a, a], axis=-1)
    sign = np.concatenate([-np.ones((half,)), np.ones((half,))])
    tile_rot = np.stack([dup(np.cos(base)), dup(np.sin(base))], axis=1)
    tile_rot = np.broadcast_to(tile_rot[:, :, None, :], (n_tiles, 2, SUBLANES, HEAD_DIM))
    cos_o, sin_o = dup(np.cos(off)), dup(np.sin(off))
    row_rot = np.stack([cos_o, sin_o, sign * cos_o, sign * sin_o])
    return _f32(tile_rot), _f32(row_rot)


def _lanes(per_head):
    return np.repeat(per_head, HEAD_DIM, axis=1)


def _decay_tables(rows, segment, causal_chunk):
    lg = _log_gamma()
    idx = np.arange(rows)
    loc = (idx % segment).astype(np.float64)
    same_seg = (idx[:, None] // segment) == (idx[None, :] // segment)
    visible = same_seg & ((idx[None, :] // causal_chunk) <= (idx[:, None] // causal_chunk))
    dist = np.abs(idx[:, None] - idx[None, :]).astype(np.float64)
    dmat = np.where(visible[None], np.exp(lg[:, None, None] * dist[None]), 0.0)
    d_in = _lanes(np.exp(lg[None, :] * (loc[:, None] + 1.0)))
    d_out = _lanes(np.exp(lg[None, :] * (segment - 1.0 - loc[:, None])))
    s_dec = np.broadcast_to(np.exp(lg * segment)[:, None], (RET_HEADS, HEAD_DIM))
    return _f32(dmat), _f32(d_in), _f32(d_out), _f32(s_dec)


def kernel(x_prompt, x_sample, state_ret, cache_conv, meta, g_pre_mix, w_in, gn_g, gn_b, dw_w, dw_b, cln_g, cln_b,
           w_out, g_post_mix, g_pre_mlp, w_mlp_in, w_mlp_out, g_post_mlp):
    batch, seq, _ = x_prompt.shape
    dec_batch, dec_seq, _ = x_sample.shape
    assert batch == 1 and dec_seq == STREAM_ROWS and meta.shape[0] == N_META == STREAM_ROWS
    assert seq % PROMPT_TILE == 0 and seq % MLP_TILE == 0 and dec_batch % SAMPLE_GROUP_STREAMS == 0
    assert (dec_batch * dec_seq) % MLP_TILE_SMALL == 0 and state_ret.shape[0] == 1

    l = 0
    post = (gn_g[l][None], gn_b[l][None], dw_w[l], dw_b[l][None], cln_g[l][None], cln_b[l][None])

    group_rows = (SAMPLE_GROUP_STREAMS + 1) * STREAM_ROWS
    pos_stream = N_META + PAST_LEN + np.arange(dec_seq)
    pos_group = np.concatenate([np.tile(pos_stream, SAMPLE_GROUP_STREAMS), np.arange(N_META)])
    cos_s, sin_s = _rope_tables(pos_group)
    dec_s = _decay_tables(group_rows, STREAM_ROWS, STREAM_ROWS)
    xs1, s_s, c_s, s_meta, c_meta, w_in_packed, w_out_packed = _sample_mixer(
        x_sample.reshape(dec_batch * dec_seq, D_MODEL), meta, cos_s, sin_s, g_pre_mix[l][None], w_in[l], *dec_s,
        *post, w_out[l], g_post_mix[l][None], state_ret[l], jnp.swapaxes(cache_conv[l], 0, 1))

    rot_p = _prompt_rope_tables(seq // RET_BLOCK, RET_BLOCK)
    dec_p = _decay_tables(RET_BLOCK, RET_BLOCK, CHUNK)
    xp1, s_p, c_p, w1_packed, w2_packed = _prompt_mixer(
        x_prompt[0], *rot_p, g_pre_mix[l][None], w_in_packed, *dec_p, *post, w_out_packed, g_post_mix[l][None],
        s_meta, c_meta, w_mlp_in[l], w_mlp_out[l])
    yp, ys = _mlp(xp1, xs1, g_pre_mlp[l][None], w1_packed, w2_packed, g_post_mlp[l][None])

    return (yp[None], ys.reshape(dec_batch, dec_seq, D_MODEL), s_p[None, None], c_p[None, None],
            s_s[None], jnp.swapaxes(c_s, 0, 1)[None])
```

```python
import functools

import jax
import jax.numpy as jnp
import numpy as np
from jax import lax
from jax.experimental import pallas as pl
from jax.experimental.pallas import tpu as pltpu

D_MODEL = 1024
CHUNK = 64
N_META = 16
PAST_LEN = 4096
RET_WIDTH = 512
RET_HEADS = 4
HEAD_DIM = 128
CONV_WIDTH = 512
CONV_K = 31
D_FF = 4096
N_IN = 4 * RET_WIDTH + 2 * CONV_WIDTH
EPS = 1e-6
ROPE_BASE = 10000.0

LANES = 128
SUBLANES = 8
HIST = 32
HIST_OFF = HIST - (CONV_K - 1)
RET_BLOCK = 256
PROMPT_TILE = 1024
STREAM_ROWS = 16
SAMPLE_GROUP_STREAMS = 8
MLP_TILE = 1024
MLP_TILE_SMALL = 512
MLP_FF_CHUNK = 1024
CONV_ROW_BLOCK = 32
CONV_LANE_BLOCKS = CONV_WIDTH // LANES
VMEM_LIMIT_BYTES = 56 * 1024 * 1024

_F32 = jnp.float32
_BF16 = jnp.bfloat16


def _rms(x, g):
    return x * lax.rsqrt(jnp.mean(x * x, axis=-1, keepdims=True) + EPS) * g


def _silu(x):
    return x * jax.nn.sigmoid(x)


def _dot(a, b):
    return jnp.dot(a, b, preferred_element_type=_F32)


def _dot_nt(a, b):
    return lax.dot_general(a, b, (((1,), (1,)), ((), ())), preferred_element_type=_F32)


def _dot_tn(a, b):
    return lax.dot_general(a, b, (((0,), (0,)), ((), ())), preferred_element_type=_F32)


def _weight(packed):
    return pltpu.bitcast(packed, _BF16)


def _head(x, h):
    return x[:, h * HEAD_DIM:(h + 1) * HEAD_DIM]


def _rope(xh, cos_f, sin_f):
    return xh * cos_f + pltpu.roll(xh, HEAD_DIM // 2, 1) * sin_f


def _split_heads(q, k, v, cos_f, sin_f):
    qs = [_rope(_head(q, hh), cos_f, sin_f) for hh in range(RET_HEADS)]
    ks = [_rope(_head(k, hh), cos_f, sin_f) * (HEAD_DIM ** -0.5) for hh in range(RET_HEADS)]
    vs = [_head(v, hh) for hh in range(RET_HEADS)]
    return qs, ks, vs


GLU_COL0 = 4 * RET_WIDTH


def _glu_cols(c):
    return slice(GLU_COL0 + 2 * c * LANES, GLU_COL0 + 2 * (c + 1) * LANES)


def _glu(pair):
    return pair[:, 0:LANES] * jax.nn.sigmoid(pair[:, LANES:2 * LANES])


def _pack_w_in(w_ref, packed_ref):
    w = pltpu.bitcast(w_ref[...].astype(_BF16), jnp.uint32)
    packed_ref[:, 0:GLU_COL0] = w[:, 0:GLU_COL0]
    for c in range(CONV_LANE_BLOCKS):
        dst = _glu_cols(c)
        packed_ref[:, dst.start:dst.start + LANES] = w[:, GLU_COL0 + c * LANES:GLU_COL0 + (c + 1) * LANES]
        packed_ref[:, dst.start + LANES:dst.stop] = w[:, GLU_COL0 + CONV_WIDTH + c * LANES:
                                                      GLU_COL0 + CONV_WIDTH + (c + 1) * LANES]


def _project(x, g_pre, w_in, cos_f, sin_f):
    h = _rms(x, g_pre).astype(_BF16)
    proj = _dot(h, w_in)
    q = proj[:, 0:RET_WIDTH]
    k = proj[:, RET_WIDTH:2 * RET_WIDTH]
    v = proj[:, 2 * RET_WIDTH:3 * RET_WIDTH]
    gate = proj[:, 3 * RET_WIDTH:4 * RET_WIDTH]
    u = jnp.concatenate([_glu(proj[:, _glu_cols(c)]) for c in range(CONV_LANE_BLOCKS)], axis=-1)
    qs, ks, vs = _split_heads(q, k, v, cos_f, sin_f)
    return qs, ks, vs, gate, u


def _group_norm_gate(ret_h, gate_h, g, b):
    mu = jnp.mean(ret_h, axis=-1, keepdims=True)
    d = ret_h - mu
    var = jnp.mean(d * d, axis=-1, keepdims=True)
    rn = d * lax.rsqrt(var + EPS) * g + b
    return rn * _silu(gate_h)


def _layer_norm(x, g, b):
    mu = jnp.mean(x, axis=-1, keepdims=True)
    d = x - mu
    var = jnp.mean(d * d, axis=-1, keepdims=True)
    return d * lax.rsqrt(var + EPS) * g + b


def _causal_conv_lanes(ext_ref, dww_ref, dwb_ref, y_ref, n_rows, c):
    rb = min(CONV_ROW_BLOCK, n_rows)
    lanes = slice(c * LANES, (c + 1) * LANES)
    taps = [dww_ref[k:k + 1, lanes] for k in range(CONV_K)]
    bias = dwb_ref[:, lanes]
    for r0 in range(0, n_rows, rb):
        acc = ext_ref[c, r0 + HIST_OFF:r0 + HIST_OFF + rb, :] * taps[0]
        for k in range(1, CONV_K):
            acc = acc + ext_ref[c, r0 + HIST_OFF + k:r0 + HIST_OFF + k + rb, :] * taps[k]
        y_ref[r0:r0 + rb, lanes] = acc + bias


def _causal_conv(ext_ref, dww_ref, dwb_ref, y_ref, n_rows):
    for c in range(CONV_LANE_BLOCKS):
        _causal_conv_lanes(ext_ref, dww_ref, dwb_ref, y_ref, n_rows, c)


def _ext_store(ext_ref, r0, rows_value):
    n = rows_value.shape[0]
    for c in range(CONV_LANE_BLOCKS):
        ext_ref[c, r0:r0 + n, :] = rows_value[:, c * LANES:(c + 1) * LANES]


def _ext_load(ext_ref, r0, n):
    return jnp.concatenate([ext_ref[c, r0:r0 + n, :] for c in range(CONV_LANE_BLOCKS)], axis=-1)


def _mix_out(x, ret_out, conv_out, wout_ref, g_post):
    half = wout_ref.shape[0] // 2
    mix = (_dot(conv_out.astype(_BF16), _weight(wout_ref[half:, :]))
           + _dot(ret_out.astype(_BF16), _weight(wout_ref[0:half, :])))
    return x + _rms(mix, g_post)


def _prompt_mixer_kernel(x_ref, tile_rot_ref, row_rot_ref, gpre_ref, win_ref, dmat_ref, din_ref, dout_ref,
                         sdec_ref, gng_ref, gnb_ref, dww_ref, dwb_ref, clng_ref, clnb_ref, wout_ref, gpost_ref,
                         s0_ref, c0_ref, w1_ref, w2_ref,
                         x1_ref, sfin_ref, cfin_ref, w1_packed_ref, w2_packed_ref,
                         s_scr, ext_scr, y_scr):
    i = pl.program_id(0)
    rows = x_ref.shape[0]

    @pl.when(i == 0)
    def _():
        s_scr[...] = s0_ref[...]
        _ext_store(ext_scr, 0, jnp.zeros((HIST_OFF, CONV_WIDTH), _F32))
        _ext_store(ext_scr, HIST_OFF, c0_ref[...])

    w1_packed_ref[...] = pltpu.bitcast(w1_ref[...].astype(_BF16), jnp.uint32)
    w2_packed_ref[...] = pltpu.bitcast(w2_ref[...].astype(_BF16), jnp.uint32)

    x = x_ref[...]
    h = _rms(x, gpre_ref[...]).astype(_BF16)

    for c in range(CONV_LANE_BLOCKS):
        ext_scr[c, HIST:HIST + rows, :] = _glu(_dot(h, _weight(win_ref[:, _glu_cols(c)])))
    blocks = []
    for c in range(CONV_LANE_BLOCKS):
        blocks.append(_dot(h, _weight(win_ref[:, c * RET_WIDTH:(c + 1) * RET_WIDTH])))
        _causal_conv_lanes(ext_scr, dww_ref, dwb_ref, y_scr, rows, c)
    q, k, v, gate = blocks
    ext_scr[:, 0:HIST, :] = ext_scr[:, rows:rows + HIST, :]
    conv_out = _silu(_layer_norm(y_scr[...], clng_ref[...], clnb_ref[...]))

    ret_rows = []
    for b in range(rows // RET_BLOCK):
        rs = slice(b * RET_BLOCK, (b + 1) * RET_BLOCK)
        cos_a, sin_a = tile_rot_ref[b, 0, 0:1, :], tile_rot_ref[b, 1, 0:1, :]
        cos_f = cos_a * row_rot_ref[0] - sin_a * row_rot_ref[1]
        sin_f = sin_a * row_rot_ref[2] + cos_a * row_rot_ref[3]
        qs, ks, vs = _split_heads(q[rs], k[rs], v[rs], cos_f, sin_f)
        ret_parts = []
        for hh in range(RET_HEADS):
            hl = slice(hh * HEAD_DIM, (hh + 1) * HEAD_DIM)
            qb = qs[hh].astype(_BF16)
            vb = vs[hh].astype(_BF16)
            p = (_dot_nt(qb, ks[hh].astype(_BF16)) * dmat_ref[hh]).astype(_BF16)
            intra = _dot(p, vb)
            s_old = s_scr[hh]
            inter = _dot(qb, s_old.astype(_BF16)) * din_ref[:, hl]
            kd = (ks[hh] * dout_ref[:, hl]).astype(_BF16)
            s_scr[hh] = s_old * sdec_ref[hh:hh + 1, :] + _dot_tn(kd, vb)
            ret_parts.append(_group_norm_gate(intra + inter, gate[rs, hl], gng_ref[:, hl], gnb_ref[:, hl]))
        ret_rows.append(jnp.concatenate(ret_parts, axis=-1))
    ret_out = jnp.concatenate(ret_rows, axis=0)

    x1_ref[...] = _mix_out(x, ret_out, conv_out, wout_ref, gpost_ref[...])

    @pl.when(i == pl.num_programs(0) - 1)
    def _():
        sfin_ref[...] = s_scr[...]
        cfin_ref[...] = _ext_load(ext_scr, HIST_OFF, CONV_K - 1)


def _const_spec(shape):
    zeros = (0,) * len(shape)
    return pl.BlockSpec(shape, lambda i, z=zeros: z)


def _resident_spec(shape):
    zeros = (0,) * len(shape)
    return pl.BlockSpec(shape, lambda i, z=zeros: z, pipeline_mode=pl.Buffered(1))


def _prompt_mixer(x, tile_rot, row_rot, g_pre, w_in, dmat, d_in, d_out, s_dec, gn_g, gn_b, dw_w, dw_b, cln_g, cln_b,
                  w_out, g_post, s0, c0, w_mlp_in, w_mlp_out):
    n = x.shape[0]
    t = PROMPT_TILE
    steps = n // t
    consts = (row_rot, g_pre, w_in, dmat, d_in, d_out, s_dec, gn_g, gn_b, dw_w, dw_b, cln_g, cln_b, w_out,
              g_post, s0, c0)
    big = (w_in, w_out, dmat)
    side = (w_mlp_in, w_mlp_out)
    slices = [w.shape[0] // steps for w in side]
    assert all(w.shape[0] % steps == 0 and r % (2 * SUBLANES) == 0 for w, r in zip(side, slices))
    in_specs = ([pl.BlockSpec((t, D_MODEL), lambda i: (i, 0)),
                 pl.BlockSpec((t // RET_BLOCK,) + tile_rot.shape[1:], lambda i: (i, 0, 0, 0))]
                + [_resident_spec(c.shape) if any(c is b for b in big) else _const_spec(c.shape) for c in consts]
                + [pl.BlockSpec((r, w.shape[1]), lambda i: (i, 0)) for w, r in zip(side, slices)])
    out_shape = (jax.ShapeDtypeStruct((n, D_MODEL), _F32),
                 jax.ShapeDtypeStruct((RET_HEADS, HEAD_DIM, HEAD_DIM), _F32),
                 jax.ShapeDtypeStruct((CONV_K - 1, CONV_WIDTH), _F32),
                 jax.ShapeDtypeStruct((w_mlp_in.shape[0] // 2, w_mlp_in.shape[1]), jnp.uint32),
                 jax.ShapeDtypeStruct((w_mlp_out.shape[0] // 2, w_mlp_out.shape[1]), jnp.uint32))
    out_specs = (pl.BlockSpec((t, D_MODEL), lambda i: (i, 0)),
                 _const_spec((RET_HEADS, HEAD_DIM, HEAD_DIM)),
                 _const_spec((CONV_K - 1, CONV_WIDTH)),
                 pl.BlockSpec((slices[0] // 2, w_mlp_in.shape[1]), lambda i: (i, 0)),
                 pl.BlockSpec((slices[1] // 2, w_mlp_out.shape[1]), lambda i: (i, 0)))
    return pl.pallas_call(
        _prompt_mixer_kernel,
        grid=(steps,),
        in_specs=in_specs,
        out_specs=out_specs,
        out_shape=out_shape,
        scratch_shapes=[pltpu.VMEM((RET_HEADS, HEAD_DIM, HEAD_DIM), _F32),
                        pltpu.VMEM((CONV_LANE_BLOCKS, HIST + t, LANES), _F32),
                        pltpu.VMEM((t, CONV_WIDTH), _F32)],
        compiler_params=pltpu.CompilerParams(dimension_semantics=("arbitrary",),
                                             vmem_limit_bytes=VMEM_LIMIT_BYTES),
        name="prompt_mixer",
    )(x, tile_rot, *consts, *side)


def _sample_mixer_kernel(x_ref, meta_ref, cos_ref, sin_ref, gpre_ref, win_ref, dmat_ref, din_ref, dout_ref, sdec_ref,
                         gng_ref, gnb_ref, dww_ref, dwb_ref, clng_ref, clnb_ref, wout_ref, gpost_ref,
                         state_ref, cache_ref,
                         x1_ref, snew_ref, cnew_ref, smeta_ref, cmeta_ref, win_packed_ref, wout_packed_ref,
                         ext_scr, y_scr, extt_scr, uslab_scr, yslab_scr):
    n_streams = state_ref.shape[0]
    stream_rows = x_ref.shape[0]

    @pl.when(pl.program_id(0) == 0)
    def _():
        _pack_w_in(win_ref, win_packed_ref)
        wout_packed_ref[...] = pltpu.bitcast(wout_ref[...].astype(_BF16), jnp.uint32)

    wout_ref = wout_packed_ref
    x = jnp.concatenate([x_ref[...], meta_ref[...]], axis=0)
    qs, ks, vs, gate, u = _project(x, gpre_ref[...], _weight(win_packed_ref[...]), cos_ref[...], sin_ref[...])

    ret_parts = []
    for h in range(RET_HEADS):
        hl = slice(h * HEAD_DIM, (h + 1) * HEAD_DIM)
        qb = qs[h].astype(_BF16)
        vb = vs[h].astype(_BF16)
        p = (_dot_nt(qb, ks[h].astype(_BF16)) * dmat_ref[h]).astype(_BF16)
        intra = _dot(p, vb)
        kd = (ks[h] * dout_ref[:, hl]).astype(_BF16)
        inter_parts = []
        for s in range(n_streams):
            rs = slice(s * STREAM_ROWS, (s + 1) * STREAM_ROWS)
            s_old = state_ref[s, h]
            inter_parts.append(_dot(qb[rs], s_old.astype(_BF16)))
            snew_ref[s, h] = s_old * sdec_ref[h:h + 1, :] + _dot_tn(kd[rs], vb[rs])
        ms = slice(stream_rows, stream_rows + STREAM_ROWS)
        inter_parts.append(jnp.zeros((STREAM_ROWS, HEAD_DIM), _F32))
        smeta_ref[h] = _dot_tn(kd[ms], vb[ms])
        inter = jnp.concatenate(inter_parts, axis=0) * din_ref[:, hl]
        ret_parts.append(_group_norm_gate(intra + inter, _head(gate, h), gng_ref[:, hl], gnb_ref[:, hl]))
    ret_out = jnp.concatenate(ret_parts, axis=-1)

    hist = CONV_K - 1
    for c in range(CONV_LANE_BLOCKS):
        uslab_scr[c] = u[0:stream_rows, c * LANES:(c + 1) * LANES]
    extt_scr[0:hist] = cache_ref[...]
    for t in range(STREAM_ROWS):
        extt_scr[hist + t] = jnp.concatenate(
            [uslab_scr[c, pl.ds(t, n_streams, stride=STREAM_ROWS), :] for c in range(CONV_LANE_BLOCKS)], axis=-1)
    for t in range(STREAM_ROWS):
        acc = extt_scr[t] * dww_ref[0:1, :]
        for k in range(1, CONV_K):
            acc = acc + extt_scr[t + k] * dww_ref[k:k + 1, :]
        y_t = acc + dwb_ref[...]
        for c in range(CONV_LANE_BLOCKS):
            yslab_scr[c, pl.ds(t, n_streams, stride=STREAM_ROWS), :] = y_t[:, c * LANES:(c + 1) * LANES]
    cnew_ref[...] = extt_scr[STREAM_ROWS:STREAM_ROWS + hist]

    _ext_store(ext_scr, 0, jnp.zeros((HIST, CONV_WIDTH), _F32))
    _ext_store(ext_scr, HIST, u[stream_rows:stream_rows + STREAM_ROWS])
    _causal_conv(ext_scr, dww_ref, dwb_ref, y_scr, STREAM_ROWS)
    cmeta_ref[...] = _ext_load(ext_scr, HIST + STREAM_ROWS - hist, hist)

    y_streams = jnp.concatenate([yslab_scr[c] for c in range(CONV_LANE_BLOCKS)], axis=-1)
    y_all = jnp.concatenate([y_streams, y_scr[...]], axis=0)
    conv_out = _silu(_layer_norm(y_all, clng_ref[...], clnb_ref[...]))

    x1_ref[...] = _mix_out(x, ret_out, conv_out, wout_ref, gpost_ref[...])[0:stream_rows]


def _sample_mixer(x_rows, meta, cos_f, sin_f, g_pre, w_in, dmat, d_in, d_out, s_dec, gn_g, gn_b, dw_w, dw_b, cln_g,
                  cln_b, w_out, g_post, state, cache):
    n = x_rows.shape[0]
    n_streams = state.shape[0]
    group = SAMPLE_GROUP_STREAMS
    rows = group * STREAM_ROWS
    consts = (meta, cos_f, sin_f, g_pre, w_in, dmat, d_in, d_out, s_dec, gn_g, gn_b, dw_w, dw_b, cln_g, cln_b, w_out,
              g_post)
    big = (w_in, w_out)
    assert group == SUBLANES
    cache_blk = pl.BlockSpec((CONV_K - 1, group, CONV_WIDTH), lambda i: (0, i, 0))
    in_specs = ([pl.BlockSpec((rows, D_MODEL), lambda i: (i, 0))]
                + [_resident_spec(c.shape) if any(c is b for b in big) else _const_spec(c.shape) for c in consts]
                + [pl.BlockSpec((group, RET_HEADS, HEAD_DIM, HEAD_DIM), lambda i: (i, 0, 0, 0)), cache_blk])
    packed = [(w.shape[0] // 2, w.shape[1]) for w in (w_in, w_out)]
    out_shape = (jax.ShapeDtypeStruct((n, D_MODEL), _F32),
                 jax.ShapeDtypeStruct((n_streams, RET_HEADS, HEAD_DIM, HEAD_DIM), _F32),
                 jax.ShapeDtypeStruct((CONV_K - 1, n_streams, CONV_WIDTH), _F32),
                 jax.ShapeDtypeStruct((RET_HEADS, HEAD_DIM, HEAD_DIM), _F32),
                 jax.ShapeDtypeStruct((CONV_K - 1, CONV_WIDTH), _F32),
                 jax.ShapeDtypeStruct(packed[0], jnp.uint32),
                 jax.ShapeDtypeStruct(packed[1], jnp.uint32))
    out_specs = (pl.BlockSpec((rows, D_MODEL), lambda i: (i, 0)),
                 pl.BlockSpec((group, RET_HEADS, HEAD_DIM, HEAD_DIM), lambda i: (i, 0, 0, 0)),
                 cache_blk,
                 _const_spec((RET_HEADS, HEAD_DIM, HEAD_DIM)),
                 _const_spec((CONV_K - 1, CONV_WIDTH)),
                 _const_spec(packed[0]),
                 _const_spec(packed[1]))
    return pl.pallas_call(
        _sample_mixer_kernel,
        grid=(n_streams // group,),
        in_specs=in_specs,
        out_specs=out_specs,
        out_shape=out_shape,
        scratch_shapes=[pltpu.VMEM((CONV_LANE_BLOCKS, HIST + STREAM_ROWS, LANES), _F32),
                        pltpu.VMEM((STREAM_ROWS, CONV_WIDTH), _F32),
                        pltpu.VMEM((CONV_K - 1 + STREAM_ROWS, group, CONV_WIDTH), _F32),
                        pltpu.VMEM((CONV_LANE_BLOCKS, rows, LANES), _F32),
                        pltpu.VMEM((CONV_LANE_BLOCKS, rows, LANES), _F32)],
        compiler_params=pltpu.CompilerParams(dimension_semantics=("arbitrary",),
                                             vmem_limit_bytes=VMEM_LIMIT_BYTES),
        name="sample_mixer",
    )(x_rows, *consts, state, cache)


def _mlp_rows(x, g_pre, w1_ref, w2_ref, g_post):
    h = _rms(x, g_pre).astype(_BF16)
    f = None
    for c0 in range(0, D_FF, MLP_FF_CHUNK):
        a = jnp.maximum(_dot(h, _weight(w1_ref[:, c0:c0 + MLP_FF_CHUNK])), 0.0)
        f_c = _dot((a * a).astype(_BF16), _weight(w2_ref[c0 // 2:(c0 + MLP_FF_CHUNK) // 2, :]))
        f = f_c if f is None else f + f_c
    return x + _rms(f, g_post)


def _mlp_kernel(xa_ref, xb_ref, gpre_ref, w1_ref, w2_ref, gpost_ref, ya_ref, yb_ref, *, a_tiles):
    i = pl.program_id(0)

    @pl.when(i < a_tiles)
    def _():
        ya_ref[...] = _mlp_rows(xa_ref[...], gpre_ref[...], w1_ref, w2_ref, gpost_ref[...])

    @pl.when(i >= a_tiles)
    def _():
        yb_ref[...] = _mlp_rows(xb_ref[...], gpre_ref[...], w1_ref, w2_ref, gpost_ref[...])


def _mlp(xa, xb, g_pre, w1, w2, g_post):
    ta, tb = MLP_TILE, MLP_TILE_SMALL
    a_tiles, b_tiles = xa.shape[0] // ta, xb.shape[0] // tb
    a_blk = lambda i: (jnp.minimum(i, a_tiles - 1), 0)
    b_blk = lambda i: (jnp.maximum(i - a_tiles, 0), 0)
    return pl.pallas_call(
        functools.partial(_mlp_kernel, a_tiles=a_tiles),
        grid=(a_tiles + b_tiles,),
        in_specs=[pl.BlockSpec((ta, D_MODEL), a_blk),
                  pl.BlockSpec((tb, D_MODEL), b_blk),
                  _const_spec(g_pre.shape),
                  _resident_spec(w1.shape),
                  _resident_spec(w2.shape),
                  _const_spec(g_post.shape)],
        out_specs=(pl.BlockSpec((ta, D_MODEL), a_blk), pl.BlockSpec((tb, D_MODEL), b_blk)),
        out_shape=(jax.ShapeDtypeStruct(xa.shape, _F32), jax.ShapeDtypeStruct(xb.shape, _F32)),
        compiler_params=pltpu.CompilerParams(dimension_semantics=("arbitrary",),
                                             vmem_limit_bytes=VMEM_LIMIT_BYTES),
        name="mlp",
    )(xa, xb, g_pre, w1, w2, g_post)


def _f32(a):
    return np.ascontiguousarray(a, dtype=np.float32)


def _log_gamma():
    return np.log1p(-np.exp2(-5.0 - np.arange(RET_HEADS, dtype=np.float64)))


def _rope_freqs():
    half = HEAD_DIM // 2
    return ROPE_BASE ** (-np.arange(half, dtype=np.float64) / half)


def _rope_tables(pos):
    ang = pos.astype(np.float64)[:, None] * _rope_freqs()[None, :]
    cos = np.cos(ang)
    sin = np.sin(ang)
    return _f32(np.concatenate([cos, cos], axis=-1)), _f32(np.concatenate([-sin, sin], axis=-1))


def _prompt_rope_tables(n_tiles, tile):
    half = HEAD_DIM // 2
    inv = _rope_freqs()
    base = (N_META + tile * np.arange(n_tiles)).astype(np.float64)[:, None] * inv[None, :]
    off = np.arange(tile).astype(np.float64)[:, None] * inv[None, :]
    dup = lambda a: np.concatenate([a, a], axis=-1)
    sign = np.concatenate([-np.ones((half,)), np.ones((half,))])
    tile_rot = np.stack([dup(np.cos(base)), dup(np.sin(base))], axis=1)
    tile_rot = np.broadcast_to(tile_rot[:, :, None, :], (n_tiles, 2, SUBLANES, HEAD_DIM))
    cos_o, sin_o = dup(np.cos(off)), dup(np.sin(off))
    row_rot = np.stack([cos_o, sin_o, sign * cos_o, sign * sin_o])
    return _f32(tile_rot), _f32(row_rot)


def _lanes(per_head):
    return np.repeat(per_head, HEAD_DIM, axis=1)


def _decay_tables(rows, segment, causal_chunk):
    lg = _log_gamma()
    idx = np.arange(rows)
    loc = (idx % segment).astype(np.float64)
    same_seg = (idx[:, None] // segment) == (idx[None, :] // segment)
    visible = same_seg & ((idx[None, :] // causal_chunk) <= (idx[:, None] // causal_chunk))
    dist = np.abs(idx[:, None] - idx[None, :]).astype(np.float64)
    dmat = np.where(visible[None], np.exp(lg[:, None, None] * dist[None]), 0.0)
    d_in = _lanes(np.exp(lg[None, :] * (loc[:, None] + 1.0)))
    d_out = _lanes(np.exp(lg[None, :] * (segment - 1.0 - loc[:, None])))
    s_dec = np.broadcast_to(np.exp(lg * segment)[:, None], (RET_HEADS, HEAD_DIM))
    return _f32(dmat), _f32(d_in), _f32(d_out), _f32(s_dec)


def kernel(x_prompt, x_sample, state_ret, cache_conv, meta, g_pre_mix, w_in, gn_g, gn_b, dw_w, dw_b, cln_g, cln_b,
           w_out, g_post_mix, g_pre_mlp, w_mlp_in, w_mlp_out, g_post_mlp):
    batch, seq, _ = x_prompt.shape
    dec_batch, dec_seq, _ = x_sample.shape
    assert batch == 1 and dec_seq == STREAM_ROWS and meta.shape[0] == N_META == STREAM_ROWS
    assert seq % PROMPT_TILE == 0 and seq % MLP_TILE == 0 and dec_batch % SAMPLE_GROUP_STREAMS == 0
    assert (dec_batch * dec_seq) % MLP_TILE_SMALL == 0 and state_ret.shape[0] == 1

    l = 0
    post = (gn_g[l][None], gn_b[l][None], dw_w[l], dw_b[l][None], cln_g[l][None], cln_b[l][None])

    group_rows = (SAMPLE_GROUP_STREAMS + 1) * STREAM_ROWS
    pos_stream = N_META + PAST_LEN + np.arange(dec_seq)
    pos_group = np.concatenate([np.tile(pos_stream, SAMPLE_GROUP_STREAMS), np.arange(N_META)])
    cos_s, sin_s = _rope_tables(pos_group)
    dec_s = _decay_tables(group_rows, STREAM_ROWS, STREAM_ROWS)
    xs1, s_s, c_s, s_meta, c_meta, w_in_packed, w_out_packed = _sample_mixer(
        x_sample.reshape(dec_batch * dec_seq, D_MODEL), meta, cos_s, sin_s, g_pre_mix[l][None], w_in[l], *dec_s,
        *post, w_out[l], g_post_mix[l][None], state_ret[l], jnp.swapaxes(cache_conv[l], 0, 1))

    rot_p = _prompt_rope_tables(seq // RET_BLOCK, RET_BLOCK)
    dec_p = _decay_tables(RET_BLOCK, RET_BLOCK, CHUNK)
    xp1, s_p, c_p, w1_packed, w2_packed = _prompt_mixer(
        x_prompt[0], *rot_p, g_pre_mix[l][None], w_in_packed, *dec_p, *post, w_out_packed, g_post_mix[l][None],
        s_meta, c_meta, w_mlp_in[l], w_mlp_out[l])
    yp, ys = _mlp(xp1, xs1, g_pre_mlp[l][None], w1_packed, w2_packed, g_post_mlp[l][None])

    return (yp[None], ys.reshape(dec_batch, dec_seq, D_MODEL), s_p[None, None], c_p[None, None],
            s_s[None], jnp.swapaxes(c_s, 0, 1)[None])
```
